```python
import jax
import jax.numpy as jnp
from jax import lax
import numpy as np

D_MODEL = 1024
BATCH = 8
SEQ = 8192
DEPTH = 4

N_MIXERS = 3
N_A = len(range(0, DEPTH, N_MIXERS))
N_B = len(range(1, DEPTH, N_MIXERS))
N_C = len(range(2, DEPTH, N_MIXERS))

HEAD_DIM = 64
MEM_LEN = 256
MEM_HEADS = 4
MEM_W = MEM_HEADS * HEAD_DIM
MIX_W = D_MODEL - MEM_W

HGRN_HEAD_DIM = 128
HGRN_HEADS = MIX_W // HGRN_HEAD_DIM
HGRN_CHUNK = 64
LB_EPS = 1e-30

SB_HEADS = MIX_W // HEAD_DIM
SB_Q_BLOCK = 128

DIL_GROUPS = ((128, 1), (512, 4), (2048, 16))
DIL_HEADS = MIX_W // HEAD_DIM
DIL_HEADS_PER_GROUP = DIL_HEADS // len(DIL_GROUPS)
DIL_OUT_W = DIL_HEADS_PER_GROUP * HEAD_DIM
DIL_BLOCK = 128
MASK_VALUE = -1e30

MOE_GROUPS = 4
MOE_EXPERTS_PER_GROUP = 4
MOE_EXPERTS = MOE_GROUPS * MOE_EXPERTS_PER_GROUP
MOE_TOP_K = 2
MOE_HIDDEN = 512

RMS_EPS = 1e-6

kernel_name = 'hybrid_hgrn2_stickbreak_dilated_hmoe'


def rmsnorm(x, gain):
    xf = x.astype(jnp.float32)
    y = xf * lax.rsqrt(jnp.mean(xf * xf, axis=-1, keepdims=True) + RMS_EPS)
    return (y * gain.astype(jnp.float32)).astype(x.dtype)


def split_heads(a, n_heads):
    b, s, _ = a.shape
    return a.reshape(b, s, n_heads, -1).transpose(0, 2, 1, 3)


def merge_heads(a):
    b, h, s, d = a.shape
    return a.transpose(0, 2, 1, 3).reshape(b, s, h * d)


def alibi_slopes(n_heads):
    return jnp.asarray(np.exp2(-8.0 * np.arange(1, n_heads + 1) / n_heads), dtype=jnp.float32)


def hgrn2_recurrence(q, k, v, log_f):
    B, H, S, Dk = q.shape
    Dv = v.shape[-1]
    nc = S // HGRN_CHUNK

    def to_chunks(a):
        return a.reshape(B, H, nc, HGRN_CHUNK, a.shape[-1]).transpose(2, 0, 1, 3, 4)

    causal = jnp.tril(jnp.ones((HGRN_CHUNK, HGRN_CHUNK), dtype=bool))[:, :, None]

    def step(state, inp):
        qc, kc, vc, gc = inp
        b = jnp.cumsum(gc, axis=2)
        diff = b[:, :, :, None, :] - b[:, :, None, :, :]
        decay = jnp.where(causal, jnp.exp(jnp.minimum(diff, 0.0)), 0.0)
        attn = jnp.einsum('bhtsd,bhsd->bhts', qc[:, :, :, None, :] * decay, kc)
        o = attn @ vc + jnp.einsum('bhtd,bhdv->bhtv', qc * jnp.exp(b), state)
        b_end = b[:, :, -1:, :]
        state = (jnp.exp(b_end[:, :, 0, :, None]) * state
                 + jnp.einsum('bhsd,bhsv->bhdv', kc * jnp.exp(b_end - b), vc))
        return state, o

    state0 = jnp.zeros((B, H, Dk, Dv), jnp.float32)
    _, o = lax.scan(step, state0, (to_chunks(q), to_chunks(k), to_chunks(v), to_chunks(log_f)))
    return o.transpose(1, 2, 0, 3, 4).reshape(B, H, S, Dv)


def hgrn2_mixer(proj, lower_bound, out_gain):
    B, S, _ = proj.shape
    q, f, i, g = jnp.split(proj, 4, axis=-1)
    q = jax.nn.silu(q.astype(jnp.float32)) * (HGRN_HEAD_DIM ** -0.5)
    lb = lower_bound.astype(jnp.float32)
    fl = f.astype(jnp.float32)
    log_f = jnp.logaddexp(jax.nn.log_sigmoid(fl), jnp.log(lb + LB_EPS) + jax.nn.log_sigmoid(-fl))
    log_f = jnp.minimum(log_f, 0.0)
    k = -jnp.expm1(log_f)
    o = hgrn2_recurrence(split_heads(q, HGRN_HEADS), split_heads(k, HGRN_HEADS),
                         split_heads(i.astype(jnp.float32), HGRN_HEADS), split_heads(log_f, HGRN_HEADS))
    o = o.transpose(0, 2, 1, 3)
    gate = jax.nn.silu(g.astype(jnp.float32)).reshape(B, S, HGRN_HEADS, HGRN_HEAD_DIM)
    o = rmsnorm(o, out_gain) * gate
    return o.reshape(B, S, MIX_W).astype(proj.dtype)


def stick_breaking_attention(q, k, v):
    B, H, S, Dh = q.shape
    scale = Dh ** -0.5
    outs = []
    for blk in range(S // SB_Q_BLOCK):
        start = blk * SB_Q_BLOCK
        stop = start + SB_Q_BLOCK
        z = jnp.einsum('bhqd,bhkd->bhqk', q[:, :, start:stop], k[:, :, :stop]).astype(jnp.float32) * scale
        q_pos = start + jnp.arange(SB_Q_BLOCK)[:, None]
        k_pos = jnp.arange(stop)[None, :]
        earlier = k_pos < q_pos
        log_keep = jnp.where(earlier, jax.nn.log_sigmoid(-z), 0.0)
        log_stick = lax.cumsum(log_keep, axis=3, reverse=True) - log_keep
        a = jnp.where(earlier, jnp.exp(jax.nn.log_sigmoid(z) + log_stick), 0.0)
        outs.append(jnp.einsum('bhqk,bhkd->bhqd', a.astype(v.dtype), v[:, :, :stop]))
    return jnp.concatenate(outs, axis=2)


def dilated_group_attention(q, k, v, dil, n_back, slopes):
    B, Hg, S, Dh = q.shape
    L = S // dil
    nb = -(-L // DIL_BLOCK)
    Lp = nb * DIL_BLOCK

    def subseq_blocks(a):
        a = a.reshape(B, Hg, L, dil, Dh).transpose(0, 1, 3, 2, 4)
        a = jnp.pad(a, ((0, 0), (0, 0), (0, 0), (0, Lp - L), (0, 0)))
        return a.reshape(B, Hg, dil, nb, DIL_BLOCK, Dh)

    def with_previous(a):
        prev = jnp.pad(a, ((0, 0), (0, 0), (0, 0), (1, 0), (0, 0), (0, 0)))[:, :, :, :nb]
        return jnp.concatenate([prev, a], axis=4)

    qb = subseq_blocks(q)
    kc = with_previous(subseq_blocks(k))
    vc = with_previous(subseq_blocks(v))
    s = jnp.einsum('bhrnqd,bhrnkd->bhrnqk', qb, kc).astype(jnp.float32) * (Dh ** -0.5)
    off = (jnp.arange(DIL_BLOCK)[:, None] + DIL_BLOCK) - jnp.arange(2 * DIL_BLOCK)[None, :]
    k_sub = jnp.arange(nb)[:, None] * DIL_BLOCK - DIL_BLOCK + jnp.arange(2 * DIL_BLOCK)[None, :]
    valid = ((off >= 0) & (off <= n_back))[None] & (k_sub >= 0)[:, None, :]
    s = s - slopes.astype(jnp.float32)[None, :, None, None, None, None] * (off * dil).astype(jnp.float32)
    s = jnp.where(valid, s, MASK_VALUE)
    m = jnp.max(s, axis=-1, keepdims=True)
    e = jnp.where(valid, jnp.exp(s - m), 0.0)
    den = jnp.sum(e, axis=-1, keepdims=True)
    o = jnp.einsum('bhrnqk,bhrnkd->bhrnqd', (e / den).astype(v.dtype), vc)
    lse = (m + jnp.log(den))[..., 0]
    o = o.reshape(B, Hg, dil, Lp, Dh)[:, :, :, :L].transpose(0, 1, 3, 2, 4).reshape(B, Hg, S, Dh)
    lse = lse.reshape(B, Hg, dil, Lp)[..., :L].transpose(0, 1, 3, 2).reshape(B, Hg, S)
    return o, lse


def dilated_mixer(proj, slopes):
    q, k, v = [split_heads(a, DIL_HEADS) for a in jnp.split(proj, 3, axis=-1)]
    outs, lses = [], []
    for g, (window, dil) in enumerate(DIL_GROUPS):
        hs = slice(g * DIL_HEADS_PER_GROUP, (g + 1) * DIL_HEADS_PER_GROUP)
        o, lse = dilated_group_attention(q[:, hs], k[:, hs], v[:, hs], dil, window // dil, slopes[hs])
        outs.append(o)
        lses.append(lse)
    w = jax.nn.softmax(jnp.stack(lses), axis=0)
    o = jnp.sum(w[..., None].astype(outs[0].dtype) * jnp.stack(outs), axis=0)
    return merge_heads(o)


def memory_attention(q_cols, mem_k, mem_v):
    q = split_heads(q_cols, MEM_HEADS)
    k = split_heads(mem_k, MEM_HEADS)
    v = split_heads(mem_v, MEM_HEADS)
    s = jnp.einsum('bhsd,bhmd->bhsm', q, k).astype(jnp.float32) * (HEAD_DIM ** -0.5)
    p = jax.nn.softmax(s, axis=-1)
    return merge_heads(jnp.einsum('bhsm,bhmd->bhsd', p.astype(v.dtype), v))


def hierarchical_moe(h, w_group, b_group, w_expert, b_expert, w1, w3, w2):
    B, S, D = h.shape
    t = h.reshape(B * S, D)
    group_logits = (t @ w_group).astype(jnp.float32) + b_group.astype(jnp.float32)
    group_prob = jax.nn.softmax(group_logits, axis=-1)
    group_idx = jnp.argmax(group_logits, axis=-1)
    group_gate = jnp.take_along_axis(group_prob, group_idx[:, None], axis=-1)
    expert_logits = ((t @ w_expert).astype(jnp.float32) + b_expert.astype(jnp.float32)
                     ).reshape(-1, MOE_GROUPS, MOE_EXPERTS_PER_GROUP)
    in_group = jnp.take_along_axis(expert_logits, group_idx[:, None, None], axis=1)[:, 0]
    top_logits, top_idx = lax.top_k(in_group, MOE_TOP_K)
    top_gate = jax.nn.softmax(top_logits, axis=-1) * group_gate
    expert_id = group_idx[:, None] * MOE_EXPERTS_PER_GROUP + top_idx
    combine = jnp.einsum('tke,tk->te', jax.nn.one_hot(expert_id, MOE_EXPERTS, dtype=jnp.float32),
                         top_gate).astype(t.dtype)
    y = jnp.zeros_like(t)
    for e in range(MOE_EXPERTS):
        hidden = jax.nn.silu(t @ w1[e]) * (t @ w3[e])
        y = y + combine[:, e:e + 1] * (hidden @ w2[e])
    return y.reshape(B, S, D)


def setup_inputs(seed: int = 0) -> dict:
    key = jax.random.key(seed)
    ks = jax.random.split(key, 24)
    f32 = jnp.float32

    def normal(k, shape, fan_in):
        return jax.random.normal(k, shape, f32) * (fan_in ** -0.5)

    def gain(k, shape):
        return 1.0 + 0.02 * jax.random.normal(k, shape, f32)

    return {
        'x': jax.random.normal(ks[0], (BATCH, SEQ, D_MODEL), f32),
        'mem': jax.random.normal(ks[1], (BATCH, MEM_LEN, D_MODEL), f32),
        'norm1': gain(ks[2], (DEPTH, D_MODEL)),
        'norm2': gain(ks[3], (DEPTH, D_MODEL)),
        'mem_norm': gain(ks[4], (D_MODEL,)),
        'final_norm': gain(ks[5], (D_MODEL,)),
        'mem_w_kv': normal(ks[6], (DEPTH, D_MODEL, 2 * MEM_W), D_MODEL),
        'hgrn_w_in': normal(ks[7], (N_A, D_MODEL, 4 * MIX_W + MEM_W), D_MODEL),
        'hgrn_onorm': gain(ks[8], (N_A, HGRN_HEAD_DIM)),
        'hgrn_lb_logits': 0.5 * jax.random.normal(ks[9], (DEPTH, MIX_W), f32),
        'hgrn_w_out': normal(ks[10], (N_A, MIX_W + MEM_W, D_MODEL), MIX_W + MEM_W),
        'sb_w_in': normal(ks[11], (N_B, D_MODEL, 3 * MIX_W + MEM_W), D_MODEL),
        'sb_w_out': normal(ks[12], (N_B, MIX_W + MEM_W, D_MODEL), MIX_W + MEM_W),
        'dil_w_in': normal(ks[13], (N_C, D_MODEL, 3 * MIX_W + MEM_W), D_MODEL),
        'dil_w_out': normal(ks[14], (N_C, DIL_OUT_W + MEM_W, D_MODEL), DIL_OUT_W + MEM_W),
        'moe_w_group': normal(ks[15], (DEPTH, D_MODEL, MOE_GROUPS), D_MODEL),
        'moe_b_group': 0.01 * jax.random.normal(ks[16], (DEPTH, MOE_GROUPS), f32),
        'moe_w_expert': normal(ks[17], (DEPTH, D_MODEL, MOE_EXPERTS), D_MODEL),
        'moe_b_expert': 0.01 * jax.random.normal(ks[18], (DEPTH, MOE_EXPERTS), f32),
        'moe_w1': normal(ks[19], (DEPTH, MOE_EXPERTS, D_MODEL, MOE_HIDDEN), D_MODEL),
        'moe_w3': normal(ks[20], (DEPTH, MOE_EXPERTS, D_MODEL, MOE_HIDDEN), D_MODEL),
        'moe_w2': normal(ks[21], (DEPTH, MOE_EXPERTS, MOE_HIDDEN, D_MODEL), MOE_HIDDEN),
    }


def reference(x, mem, norm1, norm2, mem_norm, final_norm, mem_w_kv, hgrn_w_in, hgrn_onorm,
              hgrn_lb_logits, hgrn_w_out, sb_w_in, sb_w_out, dil_w_in, dil_w_out, moe_w_group,
              moe_b_group, moe_w_expert, moe_b_expert, moe_w1, moe_w3, moe_w2):
    mem_h = rmsnorm(mem, mem_norm)
    lb_w = jax.nn.softmax(hgrn_lb_logits.astype(jnp.float32), axis=0)
    lower_bounds = jnp.maximum(jnp.cumsum(lb_w, axis=0) - lb_w[:1], 0.0)
    slopes = alibi_slopes(DIL_HEADS)
    for layer in range(DEPTH):
        kind, j = layer % N_MIXERS, layer // N_MIXERS
        h = rmsnorm(x, norm1[layer])
        mem_k, mem_v = jnp.split(mem_h @ mem_w_kv[layer], 2, axis=-1)
        if kind == 0:
            proj = h @ hgrn_w_in[j]
            mix = hgrn2_mixer(proj[..., :4 * MIX_W], lower_bounds[layer], hgrn_onorm[j])
            w_out = hgrn_w_out[j]
        elif kind == 1:
            proj = h @ sb_w_in[j]
            q, k, v = jnp.split(proj[..., :3 * MIX_W], 3, axis=-1)
            mix = merge_heads(stick_breaking_attention(split_heads(q, SB_HEADS), split_heads(k, SB_HEADS),
                                                       split_heads(v, SB_HEADS)))
            w_out = sb_w_out[j]
        else:
            proj = h @ dil_w_in[j]
            mix = dilated_mixer(proj[..., :3 * MIX_W], slopes)
            w_out = dil_w_out[j]
        mem_out = memory_attention(proj[..., proj.shape[-1] - MEM_W:], mem_k, mem_v)
        x = x + jnp.concatenate([mix, mem_out], axis=-1) @ w_out
        x = x + hierarchical_moe(rmsnorm(x, norm2[layer]), moe_w_group[layer], moe_b_group[layer],
                                 moe_w_expert[layer], moe_b_expert[layer], moe_w1[layer],
                                 moe_w3[layer], moe_w2[layer])
    return rmsnorm(x, final_norm)
```

```python
import functools

import numpy as np
import jax
import jax.numpy as jnp
from jax import lax
from jax.experimental import pallas as pl
from jax.experimental.pallas import tpu as pltpu

D_MODEL = 1024
HEAD_DIM = 64
MEM_HEADS = 4
MEM_W = MEM_HEADS * HEAD_DIM
MIX_W = D_MODEL - MEM_W
N_MIXERS = 3

HGRN_HEAD_DIM = 128
HGRN_HEADS = MIX_W // HGRN_HEAD_DIM
LB_EPS = 1e-30

DIL_GROUPS = ((128, 1), (512, 4), (2048, 16))
DIL_HEADS = MIX_W // HEAD_DIM
DIL_HEADS_PER_GROUP = DIL_HEADS // len(DIL_GROUPS)
DIL_OUT_W = DIL_HEADS_PER_GROUP * HEAD_DIM
DIL_BLOCK = 128
MASK_VALUE = -1e30

MOE_GROUPS = 4
MOE_EXPERTS_PER_GROUP = 4
MOE_EXPERTS = MOE_GROUPS * MOE_EXPERTS_PER_GROUP
MOE_HIDDEN = 512
RMS_EPS = 1e-6

LANES = 128
VMEM_LIMIT = 48 * 1024 * 1024

F32_EXP_UNDERFLOW = -104.0

BF16 = jnp.bfloat16
F32 = jnp.float32
NT_DIMS = (((1,), (1,)), ((), ()))


def _params(*semantics):
    return pltpu.CompilerParams(dimension_semantics=semantics, vmem_limit_bytes=VMEM_LIMIT)


def _log_sigmoid(z):
    return jnp.minimum(z, 0.0) - jnp.log(1.0 + jnp.exp(-jnp.abs(z)))


def _silu(z):
    return z / (1.0 + jnp.exp(-z))


def _rms(x, gain):
    return x * lax.rsqrt(jnp.mean(x * x, axis=-1, keepdims=True) + RMS_EPS) * gain


def _norm_proj_kernel(x_ref, g_ref, w_ref, *out_refs, segs):
    h = _rms(x_ref[...], g_ref[...]).astype(BF16)
    for (start, width), o_ref in zip(segs, out_refs):
        o_ref[...] = jnp.dot(h, w_ref[:, start:start + width],
                             preferred_element_type=F32).astype(o_ref.dtype)


def _norm_proj(x2, gain, w, segs, dtypes, tm=512):
    T, D = x2.shape
    N = w.shape[1]
    tm = min(tm, T)
    return pl.pallas_call(
        functools.partial(_norm_proj_kernel, segs=tuple(segs)),
        grid=(T // tm,),
        in_specs=[pl.BlockSpec((tm, D), lambda i: (i, 0)),
                  pl.BlockSpec((1, D), lambda i: (0, 0)),
                  pl.BlockSpec((D, N), lambda i: (0, 0))],
        out_specs=[pl.BlockSpec((tm, wd), lambda i: (i, 0)) for _, wd in segs],
        out_shape=[jax.ShapeDtypeStruct((T, wd), dt) for (_, wd), dt in zip(segs, dtypes)],
        compiler_params=_params("parallel"),
        name="norm_proj",
    )(x2, gain.reshape(1, D), w)


HGRN_C = 128
HGRN_LEVELS = 7


def _hgrn_level_matrix():
    C = HGRN_C
    m = np.zeros((HGRN_LEVELS + 1, C, C), np.float32)
    j = np.arange(C)
    for lvl in range(HGRN_LEVELS):
        bs = C >> lvl
        half = bs // 2
        for t in range(C):
            r = (t // bs) * bs + half - 1
            if t & half:
                m[lvl, t] = (j > r) & (j <= t)
            else:
                m[lvl, t] = (j > t) & (j <= r)
    m[HGRN_LEVELS] = j[None, :] <= j[:, None]
    return m.reshape((HGRN_LEVELS + 1) * C, C)


def _hgrn_kernel(q_ref, f_ref, i_ref, g_ref, lb_ref, gain_ref, m_ref, o_ref, st_ref, *, n_chunks):
    C = HGRN_C

    @pl.when(pl.program_id(2) == 0)
    def _():
        st_ref[...] = jnp.zeros_like(st_ref)

    row = lax.broadcasted_iota(jnp.int32, (C, C), 0)
    col = lax.broadcasted_iota(jnp.int32, (C, C), 1)
    rcol = lax.broadcasted_iota(jnp.int32, (C, 1), 0)
    log_lb = jnp.log(lb_ref[...] + LB_EPS)
    gain = gain_ref[...]

    for c in range(n_chunks):
        sl = slice(c * C, (c + 1) * C)
        z = f_ref[0, sl, :]
        a1 = _log_sigmoid(z)
        a2 = log_lb + (a1 - z)
        log_f = jnp.maximum(a1, a2) + jnp.log(1.0 + jnp.exp(-jnp.abs(a1 - a2)))
        log_f = jnp.minimum(log_f, 0.0)
        k = 1.0 - jnp.exp(log_f)
        q = _silu(q_ref[0, sl, :].astype(F32)) * (HGRN_HEAD_DIM ** -0.5)
        v = i_ref[0, sl, :]

        g_hi = log_f.astype(BF16)
        g_lo = (log_f - g_hi.astype(F32)).astype(BF16)
        e_all = jnp.dot(m_ref[...], jnp.concatenate([g_hi, g_lo], axis=1),
                        preferred_element_type=F32)
        e_all = e_all[:, :HGRN_HEAD_DIM] + e_all[:, HGRN_HEAD_DIM:]

        attn = jnp.where(row == col,
                         lax.dot_general(q.astype(BF16), k.astype(BF16), NT_DIMS,
                                         preferred_element_type=F32), 0.0)
        for lvl in range(HGRN_LEVELS):
            bs = C >> lvl
            half = bs // 2
            dec = jnp.exp(e_all[lvl * C:(lvl + 1) * C])
            upper = (rcol & half) != 0
            qt = jnp.where(upper, q * dec, 0.0).astype(BF16)
            kt = jnp.where(upper, 0.0, k * dec).astype(BF16)
            p = lax.dot_general(qt, kt, NT_DIMS, preferred_element_type=F32)
            attn = attn + jnp.where((row ^ col) < bs, p, 0.0)

        b = e_all[HGRN_LEVELS * C:]
        st = st_ref[...]
        o = jnp.dot(attn.astype(BF16), v, preferred_element_type=F32)
        o = o + lax.dot_general((q * jnp.exp(b)).astype(BF16), st.astype(BF16), NT_DIMS,
                                preferred_element_type=F32)
        b_end = b[C - 1:C, :]
        k_end = (k * jnp.exp(b_end - b)).astype(BF16)
        v_t = v.astype(F32).T.astype(BF16)
        st_ref[...] = st * jnp.exp(b_end) + jnp.dot(v_t, k_end, preferred_element_type=F32)

        gate = _silu(g_ref[0, sl, :].astype(F32))
        o_ref[0, sl, :] = (_rms(o, gain) * gate).astype(o_ref.dtype)


def _hgrn_mixer(q, f, i, g, lower_bound, out_gain, ts=256):
    B, S, W = q.shape
    dh = HGRN_HEAD_DIM
    ts = min(ts, S)
    seq = pl.BlockSpec((1, ts, dh), lambda b, h, s: (b, s, h))
    m = jnp.asarray(_hgrn_level_matrix(), BF16)
    return pl.pallas_call(
        functools.partial(_hgrn_kernel, n_chunks=ts // HGRN_C),
        grid=(B, W // dh, S // ts),
        in_specs=[seq, seq, seq, seq,
                  pl.BlockSpec((1, dh), lambda b, h, s: (0, h)),
                  pl.BlockSpec((1, dh), lambda b, h, s: (0, 0)),
                  pl.BlockSpec(m.shape, lambda b, h, s: (0, 0))],
        out_specs=seq,
        out_shape=jax.ShapeDtypeStruct((B, S, W), BF16),
        scratch_shapes=[pltpu.VMEM((dh, dh), F32)],
        compiler_params=_params("parallel", "parallel", "arbitrary"),
        name="hgrn_mixer",
    )(q, f, i, g, lower_bound.reshape(1, W), out_gain.reshape(1, dh), m)


SB_BLOCK = 128


def _sb_kernel(q_ref, k_ref, v_ref, o_ref, *, n_blocks):
    BQ = SB_BLOCK
    row = lax.broadcasted_iota(jnp.int32, (BQ, BQ), 0)
    col = lax.broadcasted_iota(jnp.int32, (BQ, BQ), 1)
    later = jnp.where(row > col, 1.0, 0.0).astype(BF16)

    def q_block(qb, _):
        q0 = pl.multiple_of(qb * BQ, BQ)
        outs = []
        for hh in range(LANES // HEAD_DIM):
            hs = slice(hh * HEAD_DIM, (hh + 1) * HEAD_DIM)
            q = q_ref[0, pl.ds(q0, BQ), hs] * (HEAD_DIM ** -0.5)

            def cond(state):
                j, _, _, go = state
                return jnp.logical_and(j <= qb, go > 0)

            def body(state):
                j, acc, c, _ = state
                k0 = pl.multiple_of((qb - j) * BQ, BQ)
                k = k_ref[0, pl.ds(k0, BQ), hs]
                v = v_ref[0, pl.ds(k0, BQ), hs]
                z = lax.dot_general(q, k, NT_DIMS, preferred_element_type=F32)
                earlier = (k0 + col) < (q0 + row)
                ls = _log_sigmoid(z)
                lk = jnp.where(earlier, ls - z, 0.0)
                lk_hi = lk.astype(BF16)
                lk_lo = (lk - lk_hi.astype(F32)).astype(BF16)
                stick = (jnp.dot(lk_hi, later, preferred_element_type=F32)
                         + jnp.dot(lk_lo, later, preferred_element_type=F32) + c)
                a = jnp.where(earlier, jnp.exp(ls + stick), 0.0)
                acc = acc + jnp.dot(a.astype(BF16), v, preferred_element_type=F32)
                c = c + jnp.sum(lk, axis=1, keepdims=True)
                go = (jnp.max(c) > F32_EXP_UNDERFLOW).astype(jnp.int32)
                return j + 1, acc, c, go

            init = (jnp.int32(0), jnp.zeros((BQ, HEAD_DIM), F32), jnp.zeros((BQ, 1), F32), jnp.int32(1))
            outs.append(lax.while_loop(cond, body, init)[1])
        o_ref[0, pl.ds(q0, BQ), :] = jnp.concatenate(outs, axis=1).astype(o_ref.dtype)
        return 0

    lax.fori_loop(0, n_blocks, q_block, 0)


def _sb_mixer(q, k, v):
    B, S, W = q.shape
    spec = pl.BlockSpec((1, S, LANES), lambda b, h: (b, 0, h))
    return pl.pallas_call(
        functools.partial(_sb_kernel, n_blocks=S // SB_BLOCK),
        grid=(B, W // LANES),
        in_specs=[spec, spec, spec],
        out_specs=spec,
        out_shape=jax.ShapeDtypeStruct((B, S, W), BF16),
        compiler_params=_params("parallel", "parallel"),
        name="sb_mixer",
    )(q, k, v)


def _alibi_slope(head):
    return float(np.exp2(-8.0 * (head + 1) / DIL_HEADS))


def _dil_kernel(q_ref, kp_ref, kc_ref, vp_ref, vc_ref, o_ref, l_ref, *, group, dil, n_back):
    BQ = DIL_BLOCK
    n = pl.program_id(2)
    row = lax.broadcasted_iota(jnp.int32, (BQ, BQ), 0)
    col = lax.broadcasted_iota(jnp.int32, (BQ, BQ), 1)
    off_c = row - col
    off_p = off_c + BQ
    valid_c = (off_c >= 0) & (off_c <= n_back)
    valid_p = (off_p <= n_back) & (n > 0)
    outs, lses = [], []
    for hh in range(DIL_HEADS_PER_GROUP):
        hs = slice(hh * HEAD_DIM, (hh + 1) * HEAD_DIM)
        slope = _alibi_slope(group * DIL_HEADS_PER_GROUP + hh) * dil
        q = q_ref[0, :, hs] * (HEAD_DIM ** -0.5)
        s_c = lax.dot_general(q, kc_ref[0, :, hs], NT_DIMS, preferred_element_type=F32)
        s_p = lax.dot_general(q, kp_ref[0, :, hs], NT_DIMS, preferred_element_type=F32)
        s_c = jnp.where(valid_c, s_c - slope * off_c.astype(F32), MASK_VALUE)
        s_p = jnp.where(valid_p, s_p - slope * off_p.astype(F32), MASK_VALUE)
        m = jnp.maximum(jnp.max(s_c, axis=1, keepdims=True), jnp.max(s_p, axis=1, keepdims=True))
        e_c = jnp.where(valid_c, jnp.exp(s_c - m), 0.0)
        e_p = jnp.where(valid_p, jnp.exp(s_p - m), 0.0)
        den = jnp.sum(e_c, axis=1, keepdims=True) + jnp.sum(e_p, axis=1, keepdims=True)
        o = (jnp.dot(e_c.astype(BF16), vc_ref[0, :, hs], preferred_element_type=F32)
             + jnp.dot(e_p.astype(BF16), vp_ref[0, :, hs], preferred_element_type=F32))
        outs.append(o / den)
        lses.append(jnp.broadcast_to(m + jnp.log(den), (BQ, HEAD_DIM)))
    o_ref[0] = jnp.concatenate(outs, axis=1).astype(o_ref.dtype)
    l_ref[0] = jnp.concatenate(lses, axis=1)


def _dil_group(q, k, v, group):
    window, dil = DIL_GROUPS[group]
    B, S, W = q.shape
    L = S // dil
    nb = L // DIL_BLOCK
    gpr = W // DIL_OUT_W
    view = lambda a: a.reshape(B, L, dil * W)
    cur = pl.BlockSpec((1, DIL_BLOCK, DIL_OUT_W), lambda b, r, n: (b, n, r * gpr + group))
    prev = pl.BlockSpec((1, DIL_BLOCK, DIL_OUT_W),
                        lambda b, r, n: (b, jnp.maximum(n - 1, 0), r * gpr + group))
    out = pl.BlockSpec((1, DIL_BLOCK, DIL_OUT_W), lambda b, r, n: (b, n, r))
    o, lse = pl.pallas_call(
        functools.partial(_dil_kernel, group=group, dil=dil, n_back=window // dil),
        grid=(B, dil, nb),
        in_specs=[cur, prev, cur, prev, cur],
        out_specs=[out, out],
        out_shape=[jax.ShapeDtypeStruct((B, L, dil * DIL_OUT_W), BF16),
                   jax.ShapeDtypeStruct((B, L, dil * DIL_OUT_W), F32)],
        compiler_params=_params("parallel", "parallel", "parallel"),
        name=f"dil_group{group}",
    )(view(q), view(k), view(k), view(v), view(v))
    return o.reshape(B, S, DIL_OUT_W), lse.reshape(B, S, DIL_OUT_W)


def _mem_attn_kernel(q_ref, k_ref, v_ref, o_ref):
    outs = []
    for hh in range(MEM_HEADS):
        hs = slice(hh * HEAD_DIM, (hh + 1) * HEAD_DIM)
        q = q_ref[0, :, hs] * (HEAD_DIM ** -0.5)
        s = lax.dot_general(q, k_ref[0, :, hs], NT_DIMS, preferred_element_type=F32)
        e = jnp.exp(s - jnp.max(s, axis=1, keepdims=True))
        o = jnp.dot(e.astype(BF16), v_ref[0, :, hs], preferred_element_type=F32)
        outs.append(o / jnp.sum(e, axis=1, keepdims=True))
    o_ref[0] = jnp.concatenate(outs, axis=1).astype(o_ref.dtype)


def _mem_attention(q, mem_k, mem_v, tm=1024):
    B, S, W = q.shape
    M = mem_k.shape[1]
    tm = min(tm, S)
    tok = pl.BlockSpec((1, tm, W), lambda b, s: (b, s, 0))
    kv = pl.BlockSpec((1, M, W), lambda b, s: (b, 0, 0))
    return pl.pallas_call(
        _mem_attn_kernel,
        grid=(B, S // tm),
        in_specs=[tok, kv, kv],
        out_specs=tok,
        out_shape=jax.ShapeDtypeStruct((B, S, W), BF16),
        compiler_params=_params("parallel", "parallel"),
        name="mem_attention",
    )(q, mem_k, mem_v)


def _out_proj_kernel(mix_ref, mem_ref, x_ref, w_ref, o_ref):
    nm = mix_ref.shape[-1]
    o_ref[...] = (x_ref[...]
                  + jnp.dot(mix_ref[...], w_ref[:nm, :], preferred_element_type=F32)
                  + jnp.dot(mem_ref[...], w_ref[nm:, :], preferred_element_type=F32))


def _dil_out_proj_kernel(o0_ref, o1_ref, o2_ref, l0_ref, l1_ref, l2_ref, mem_ref, x_ref, w_ref, o_ref):
    l0, l1, l2 = l0_ref[...], l1_ref[...], l2_ref[...]
    m = jnp.maximum(jnp.maximum(l0, l1), l2)
    w0, w1, w2 = jnp.exp(l0 - m), jnp.exp(l1 - m), jnp.exp(l2 - m)
    mix = (w0 * o0_ref[...].astype(F32) + w1 * o1_ref[...].astype(F32)
           + w2 * o2_ref[...].astype(F32)) / (w0 + w1 + w2)
    nm = mix.shape[-1]
    o_ref[...] = (x_ref[...]
                  + jnp.dot(mix.astype(BF16), w_ref[:nm, :], preferred_element_type=F32)
                  + jnp.dot(mem_ref[...], w_ref[nm:, :], preferred_element_type=F32))


def _out_proj(mix_parts, mem_out, x2, w_out, tm=512):
    T, D = x2.shape
    tm = min(tm, T)
    tok = lambda a: pl.BlockSpec((tm, a.shape[1]), lambda i: (i, 0))
    kern = _out_proj_kernel if len(mix_parts) == 1 else _dil_out_proj_kernel
    ins = list(mix_parts) + [mem_out, x2]
    return pl.pallas_call(
        kern,
        grid=(T // tm,),
        in_specs=[tok(a) for a in ins] + [pl.BlockSpec(w_out.shape, lambda i: (0, 0))],
        out_specs=pl.BlockSpec((tm, D), lambda i: (i, 0)),
        out_shape=jax.ShapeDtypeStruct((T, D), F32),
        compiler_params=_params("parallel"),
        name="out_proj",
    )(*ins, w_out)


def _router_kernel(x_ref, g_ref, w_ref, b_ref, h_ref, c_ref):
    h = _rms(x_ref[...], g_ref[...])
    h_ref[...] = h.astype(h_ref.dtype)
    logits = jnp.dot(h, w_ref[...], preferred_element_type=F32,
                     precision=lax.Precision.HIGHEST) + b_ref[...]
    lane = lax.broadcasted_iota(jnp.int32, logits.shape, 1)
    neg = jnp.float32(-1e30)

    def first_argmax(vals):
        mx = jnp.max(vals, axis=1, keepdims=True)
        return mx, jnp.min(jnp.where(vals == mx, lane, LANES), axis=1, keepdims=True)

    glog = jnp.where(lane < MOE_GROUPS, logits, neg)
    gmax, gidx = first_argmax(glog)
    group_gate = 1.0 / jnp.sum(jnp.where(lane < MOE_GROUPS, jnp.exp(glog - gmax), 0.0),
                               axis=1, keepdims=True)
    lo = MOE_GROUPS + gidx * MOE_EXPERTS_PER_GROUP
    elog = jnp.where((lane >= lo) & (lane < lo + MOE_EXPERTS_PER_GROUP), logits, neg)
    m1, i1 = first_argmax(elog)
    m2, i2 = first_argmax(jnp.where(lane == i1, neg, elog))
    p1 = 1.0 / (1.0 + jnp.exp(m2 - m1))
    p2 = 1.0 - p1
    c_ref[...] = (jnp.where(lane == i1, p1 * group_gate, 0.0)
                  + jnp.where(lane == i2, p2 * group_gate, 0.0))


def _router(x2, gain, w_router, b_router, tm=512):
    T, D = x2.shape
    tm = min(tm, T)
    return pl.pallas_call(
        _router_kernel,
        grid=(T // tm,),
        in_specs=[pl.BlockSpec((tm, D), lambda i: (i, 0)),
                  pl.BlockSpec((1, D), lambda i: (0, 0)),
                  pl.BlockSpec((D, LANES), lambda i: (0, 0)),
                  pl.BlockSpec((1, LANES), lambda i: (0, 0))],
        out_specs=[pl.BlockSpec((tm, D), lambda i: (i, 0)),
                   pl.BlockSpec((tm, LANES), lambda i: (i, 0))],
        out_shape=[jax.ShapeDtypeStruct((T, D), BF16), jax.ShapeDtypeStruct((T, LANES), F32)],
        compiler_params=_params("parallel"),
        name="moe_router",
    )(x2, gain.reshape(1, D), w_router, b_router)


def _moe_kernel(h_ref, c_ref, x_ref, w1_ref, w3_ref, w2_ref, o_ref, acc_ref):
    e = pl.program_id(1)

    @pl.when(e == 0)
    def _():
        acc_ref[...] = x_ref[...]

    h = h_ref[...]
    a = jnp.dot(h, w1_ref[0], preferred_element_type=F32)
    b = jnp.dot(h, w3_ref[0], preferred_element_type=F32)
    hidden = (_silu(a) * b).astype(BF16)
    y = jnp.dot(hidden, w2_ref[0], preferred_element_type=F32)
    lane = lax.broadcasted_iota(jnp.int32, c_ref.shape, 1)
    gate = jnp.sum(jnp.where(lane == MOE_GROUPS + e, c_ref[...], 0.0), axis=1, keepdims=True)
    acc_ref[...] += gate * y

    @pl.when(e == pl.num_programs(1) - 1)
    def _():
        o_ref[...] = acc_ref[...]


def _moe_experts(h, combine, x2, w1, w3, w2, tm=512):
    T, D = x2.shape
    E, _, H = w1.shape
    tm = min(tm, T)
    tok = lambda wd: pl.BlockSpec((tm, wd), lambda i, e: (i, 0))
    return pl.pallas_call(
        _moe_kernel,
        grid=(T // tm, E),
        in_specs=[tok(D), tok(LANES), tok(D),
                  pl.BlockSpec((1, D, H), lambda i, e: (e, 0, 0)),
                  pl.BlockSpec((1, D, H), lambda i, e: (e, 0, 0)),
                  pl.BlockSpec((1, H, D), lambda i, e: (e, 0, 0))],
        out_specs=tok(D),
        out_shape=jax.ShapeDtypeStruct((T, D), F32),
        scratch_shapes=[pltpu.VMEM((tm, D), F32)],
        compiler_params=_params("parallel", "arbitrary"),
        name="moe_experts",
    )(h, combine, x2, w1, w3, w2)


def _final_norm_kernel(x_ref, g_ref, o_ref):
    o_ref[...] = _rms(x_ref[...], g_ref[...])


def _final_norm(x2, gain, tm=1024):
    T, D = x2.shape
    tm = min(tm, T)
    return pl.pallas_call(
        _final_norm_kernel,
        grid=(T // tm,),
        in_specs=[pl.BlockSpec((tm, D), lambda i: (i, 0)), pl.BlockSpec((1, D), lambda i: (0, 0))],
        out_specs=pl.BlockSpec((tm, D), lambda i: (i, 0)),
        out_shape=jax.ShapeDtypeStruct((T, D), F32),
        compiler_params=_params("parallel"),
        name="final_norm",
    )(x2, gain.reshape(1, D))


def _segments(widths):
    starts = np.cumsum([0] + list(widths[:-1]))
    return [(int(s), int(w)) for s, w in zip(starts, widths)]


def kernel(x, mem, norm1, norm2, mem_norm, final_norm, mem_w_kv, hgrn_w_in, hgrn_onorm, hgrn_lb_logits, hgrn_w_out, sb_w_in, sb_w_out, dil_w_in, dil_w_out, moe_w_group, moe_b_group, moe_w_expert, moe_b_expert, moe_w1, moe_w3, moe_w2):
    B, S, D = x.shape
    depth = norm1.shape[0]
    T = B * S
    x2 = x.reshape(T, D)

    lb_w = jax.nn.softmax(hgrn_lb_logits.astype(F32), axis=0)
    lower_bounds = jnp.maximum(jnp.cumsum(lb_w, axis=0) - lb_w[:1], 0.0)

    M = mem.shape[1]
    w_kv = mem_w_kv.transpose(1, 0, 2).reshape(D, depth * 2 * MEM_W).astype(BF16)
    mem_kv = _norm_proj(mem.reshape(B * M, D), mem_norm, w_kv, _segments([MEM_W] * (2 * depth)),
                        [BF16] * (2 * depth))

    for layer in range(depth):
        kind, j = layer % N_MIXERS, layer // N_MIXERS
        mem_k = mem_kv[2 * layer].reshape(B, M, MEM_W)
        mem_v = mem_kv[2 * layer + 1].reshape(B, M, MEM_W)
        seq = lambda a: a.reshape(B, S, a.shape[-1])
        if kind == 0:
            q, f, i, g, mq = _norm_proj(x2, norm1[layer], hgrn_w_in[j].astype(BF16),
                                        _segments([MIX_W] * 4 + [MEM_W]), [BF16, F32, BF16, BF16, BF16])
            mix = _hgrn_mixer(seq(q), seq(f), seq(i), seq(g), lower_bounds[layer], hgrn_onorm[j])
            mix_parts, w_out = [mix.reshape(T, MIX_W)], hgrn_w_out[j]
        elif kind == 1:
            q, k, v, mq = _norm_proj(x2, norm1[layer], sb_w_in[j].astype(BF16),
                                     _segments([MIX_W] * 3 + [MEM_W]), [BF16] * 4)
            mix = _sb_mixer(seq(q), seq(k), seq(v))
            mix_parts, w_out = [mix.reshape(T, MIX_W)], sb_w_out[j]
        else:
            q, k, v, mq = _norm_proj(x2, norm1[layer], dil_w_in[j].astype(BF16),
                                     _segments([MIX_W] * 3 + [MEM_W]), [BF16] * 4)
            parts = [_dil_group(seq(q), seq(k), seq(v), grp) for grp in range(len(DIL_GROUPS))]
            mix_parts = ([o.reshape(T, DIL_OUT_W) for o, _ in parts]
                         + [l.reshape(T, DIL_OUT_W) for _, l in parts])
            w_out = dil_w_out[j]
        mem_out = _mem_attention(seq(mq), mem_k, mem_v).reshape(T, MEM_W)
        x2 = _out_proj(mix_parts, mem_out, x2, w_out.astype(BF16))

        pad = LANES - MOE_GROUPS - MOE_EXPERTS
        w_router = jnp.pad(jnp.concatenate([moe_w_group[layer], moe_w_expert[layer]], axis=1),
                           ((0, 0), (0, pad))).astype(F32)
        b_router = jnp.pad(jnp.concatenate([moe_b_group[layer], moe_b_expert[layer]]),
                           (0, pad)).astype(F32).reshape(1, LANES)
        h, combine = _router(x2, norm2[layer], w_router, b_router)
        x2 = _moe_experts(h, combine, x2, moe_w1[layer].astype(BF16), moe_w3[layer].astype(BF16),
                          moe_w2[layer].astype(BF16))

    return _final_norm(x2, final_norm).reshape(B, S, D)
```

```python
import functools

import numpy as np
import jax
import jax.numpy as jnp
from jax import lax
from jax.experimental import pallas as pl
from jax.experimental.pallas import tpu as pltpu

D_MODEL = 1024
HEAD_DIM = 64
MEM_HEADS = 4
MEM_W = MEM_HEADS * HEAD_DIM
MIX_W = D_MODEL - MEM_W
N_MIXERS = 3

HGRN_HEAD_DIM = 128
HGRN_HEADS = MIX_W // HGRN_HEAD_DIM
LB_EPS = 1e-30

DIL_GROUPS = ((128, 1), (512, 4), (2048, 16))
DIL_HEADS = MIX_W // HEAD_DIM
DIL_HEADS_PER_GROUP = DIL_HEADS // len(DIL_GROUPS)
DIL_OUT_W = DIL_HEADS_PER_GROUP * HEAD_DIM
DIL_BLOCK = 128
MASK_VALUE = -1e30

MOE_GROUPS = 4
MOE_EXPERTS_PER_GROUP = 4
MOE_EXPERTS = MOE_GROUPS * MOE_EXPERTS_PER_GROUP
MOE_HIDDEN = 512
RMS_EPS = 1e-6

LANES = 128
VMEM_LIMIT = 48 * 1024 * 1024

F32_EXP_UNDERFLOW = -104.0

BF16 = jnp.bfloat16
F32 = jnp.float32
NT_DIMS = (((1,), (1,)), ((), ()))


def _params(*semantics):
    return pltpu.CompilerParams(dimension_semantics=semantics, vmem_limit_bytes=VMEM_LIMIT)


def _log_sigmoid(z):
    return jnp.minimum(z, 0.0) - jnp.log(1.0 + jnp.exp(-jnp.abs(z)))


def _silu(z):
    return z / (1.0 + jnp.exp(-z))


def _rms(x, gain):
    return x * lax.rsqrt(jnp.mean(x * x, axis=-1, keepdims=True) + RMS_EPS) * gain


def _norm_proj_kernel(x_ref, g_ref, w_ref, *out_refs, segs):
    h = _rms(x_ref[...], g_ref[...]).astype(BF16)
    for (start, width), o_ref in zip(segs, out_refs):
        o_ref[...] = jnp.dot(h, w_ref[:, start:start + width],
                             preferred_element_type=F32).astype(o_ref.dtype)


def _norm_proj(x2, gain, w, segs, dtypes, tm=512):
    T, D = x2.shape
    N = w.shape[1]
    tm = min(tm, T)
    return pl.pallas_call(
        functools.partial(_norm_proj_kernel, segs=tuple(segs)),
        grid=(T // tm,),
        in_specs=[pl.BlockSpec((tm, D), lambda i: (i, 0)),
                  pl.BlockSpec((1, D), lambda i: (0, 0)),
                  pl.BlockSpec((D, N), lambda i: (0, 0))],
        out_specs=[pl.BlockSpec((tm, wd), lambda i: (i, 0)) for _, wd in segs],
        out_shape=[jax.ShapeDtypeStruct((T, wd), dt) for (_, wd), dt in zip(segs, dtypes)],
        compiler_params=_params("parallel"),
        name="norm_proj",
    )(x2, gain.reshape(1, D), w)


HGRN_C = 128
HGRN_LEVELS = 7


def _hgrn_level_matrix():
    C = HGRN_C
    m = np.zeros((HGRN_LEVELS + 1, C, C), np.float32)
    j = np.arange(C)
    for lvl in range(HGRN_LEVELS):
        bs = C >> lvl
        half = bs // 2
        for t in range(C):
            r = (t // bs) * bs + half - 1
            if t & half:
                m[lvl, t] = (j > r) & (j <= t)
            else:
                m[lvl, t] = (j > t) & (j <= r)
    m[HGRN_LEVELS] = j[None, :] <= j[:, None]
    return m.reshape((HGRN_LEVELS + 1) * C, C)


def _hgrn_kernel(q_ref, f_ref, i_ref, g_ref, lb_ref, gain_ref, m_ref, o_ref, st_ref, *, n_chunks):
    C = HGRN_C

    @pl.when(pl.program_id(2) == 0)
    def _():
        st_ref[...] = jnp.zeros_like(st_ref)

    row = lax.broadcasted_iota(jnp.int32, (C, C), 0)
    col = lax.broadcasted_iota(jnp.int32, (C, C), 1)
    rcol = lax.broadcasted_iota(jnp.int32, (C, 1), 0)
    log_lb = jnp.log(lb_ref[...] + LB_EPS)
    gain = gain_ref[...]

    for c in range(n_chunks):
        sl = slice(c * C, (c + 1) * C)
        z = f_ref[0, sl, :]
        a1 = _log_sigmoid(z)
        a2 = log_lb + (a1 - z)
        log_f = jnp.maximum(a1, a2) + jnp.log(1.0 + jnp.exp(-jnp.abs(a1 - a2)))
        log_f = jnp.minimum(log_f, 0.0)
        k = 1.0 - jnp.exp(log_f)
        q = _silu(q_ref[0, sl, :].astype(F32)) * (HGRN_HEAD_DIM ** -0.5)
        v = i_ref[0, sl, :]

        g_hi = log_f.astype(BF16)
        g_lo = (log_f - g_hi.astype(F32)).astype(BF16)
        e_all = jnp.dot(m_ref[...], jnp.concatenate([g_hi, g_lo], axis=1),
                        preferred_element_type=F32)
        e_all = e_all[:, :HGRN_HEAD_DIM] + e_all[:, HGRN_HEAD_DIM:]

        attn = jnp.where(row == col,
                         lax.dot_general(q.astype(BF16), k.astype(BF16), NT_DIMS,
                                         preferred_element_type=F32), 0.0)
        for lvl in range(HGRN_LEVELS):
            bs = C >> lvl
            half = bs // 2
            dec = jnp.exp(e_all[lvl * C:(lvl + 1) * C])
            upper = (rcol & half) != 0
            qt = jnp.where(upper, q * dec, 0.0).astype(BF16)
            kt = jnp.where(upper, 0.0, k * dec).astype(BF16)
            p = lax.dot_general(qt, kt, NT_DIMS, preferred_element_type=F32)
            attn = attn + jnp.where((row ^ col) < bs, p, 0.0)

        b = e_all[HGRN_LEVELS * C:]
        st = st_ref[...]
        o = jnp.dot(attn.astype(BF16), v, preferred_element_type=F32)
        o = o + lax.dot_general((q * jnp.exp(b)).astype(BF16), st.astype(BF16), NT_DIMS,
                                preferred_element_type=F32)
        b_end = b[C - 1:C, :]
        k_end = (k * jnp.exp(b_end - b)).astype(BF16)
        v_t = v.astype(F32).T.astype(BF16)
        st_ref[...] = st * jnp.exp(b_end) + jnp.dot(v_t, k_end, preferred_element_type=F32)

        gate = _silu(g_ref[0, sl, :].astype(F32))
        o_ref[0, sl, :] = (_rms(o, gain) * gate).astype(o_ref.dtype)


def _hgrn_mixer(q, f, i, g, lower_bound, out_gain, ts=512):
    B, S, W = q.shape
    dh = HGRN_HEAD_DIM
    ts = min(ts, S)
    seq = pl.BlockSpec((1, ts, dh), lambda b, h, s: (b, s, h))
    m = jnp.asarray(_hgrn_level_matrix(), BF16)
    return pl.pallas_call(
        functools.partial(_hgrn_kernel, n_chunks=ts // HGRN_C),
        grid=(B, W // dh, S // ts),
        in_specs=[seq, seq, seq, seq,
                  pl.BlockSpec((1, dh), lambda b, h, s: (0, h)),
                  pl.BlockSpec((1, dh), lambda b, h, s: (0, 0)),
                  pl.BlockSpec(m.shape, lambda b, h, s: (0, 0))],
        out_specs=seq,
        out_shape=jax.ShapeDtypeStruct((B, S, W), BF16),
        scratch_shapes=[pltpu.VMEM((dh, dh), F32)],
        compiler_params=_params("parallel", "parallel", "arbitrary"),
        name="hgrn_mixer",
    )(q, f, i, g, lower_bound.reshape(1, W), out_gain.reshape(1, dh), m)


SB_BLOCK = 256
SB_HALF = LANES


def _sb_kernel(q_ref, k_ref, v_ref, o_ref, *, n_blocks):
    BQ, HB = SB_BLOCK, SB_HALF
    heads = [slice(h * HEAD_DIM, (h + 1) * HEAD_DIM) for h in range(LANES // HEAD_DIM)]
    row = lax.broadcasted_iota(jnp.int32, (BQ, BQ), 0)
    col = lax.broadcasted_iota(jnp.int32, (BQ, BQ), 1)
    tr = lax.broadcasted_iota(jnp.int32, (HB, 2 * HB), 0)
    tc = lax.broadcasted_iota(jnp.int32, (HB, 2 * HB), 1)
    suffix = jnp.where((tc >= HB) | (tr > tc), 1.0, 0.0).astype(BF16)

    def step(q, k0, hs, acc, c, diagonal):
        k = k_ref[0, pl.ds(k0, BQ), hs]
        v = v_ref[0, pl.ds(k0, BQ), hs]
        z = lax.dot_general(q, k, NT_DIMS, preferred_element_type=F32)
        ls = _log_sigmoid(z)
        lk = ls - z
        if diagonal:
            earlier = col < row
            lk = jnp.where(earlier, lk, 0.0)
        lk_hi = lk.astype(BF16)
        lk_lo = (lk - lk_hi.astype(F32)).astype(BF16)
        lhs = jnp.concatenate([lk_hi[:, HB:], lk_lo[:, HB:], lk_hi[:, :HB], lk_lo[:, :HB]], axis=0)
        res = jnp.dot(lhs, suffix, preferred_element_type=F32)
        res_r = res[:BQ] + res[BQ:2 * BQ]
        res_l = res[2 * BQ:3 * BQ] + res[3 * BQ:]
        stick_r = res_r[:, :HB] + c
        c = c + res_r[:, HB:]
        stick_l = res_l[:, :HB] + c
        c = c + res_l[:, HB:]
        a = jnp.exp(ls + jnp.concatenate([stick_l, stick_r], axis=1))
        if diagonal:
            a = jnp.where(earlier, a, 0.0)
        return acc + jnp.dot(a.astype(BF16), v, preferred_element_type=F32), c

    def keep_going(c0, c1):
        return (jnp.max(jnp.maximum(c0, c1)) > F32_EXP_UNDERFLOW).astype(jnp.int32)

    def q_block(qb, _):
        q0 = pl.multiple_of(qb * BQ, BQ)
        qs = [q_ref[0, pl.ds(q0, BQ), hs] * (HEAD_DIM ** -0.5) for hs in heads]
        (acc0, c0), (acc1, c1) = [
            step(q, q0, hs, jnp.zeros((BQ, HEAD_DIM), F32), jnp.zeros((BQ, HB), F32), True)
            for q, hs in zip(qs, heads)]

        def cond(state):
            return jnp.logical_and(state[0] <= qb, state[-1] > 0)

        def body(state):
            j, acc0, acc1, c0, c1, _ = state
            k0 = pl.multiple_of((qb - j) * BQ, BQ)
            acc0, c0 = step(qs[0], k0, heads[0], acc0, c0, False)
            acc1, c1 = step(qs[1], k0, heads[1], acc1, c1, False)
            return j + 1, acc0, acc1, c0, c1, keep_going(c0, c1)

        state = lax.while_loop(cond, body, (jnp.int32(1), acc0, acc1, c0, c1, keep_going(c0, c1)))
        o_ref[0, pl.ds(q0, BQ), :] = jnp.concatenate(state[1:3], axis=1).astype(o_ref.dtype)
        return 0

    lax.fori_loop(0, n_blocks, q_block, 0)


def _sb_mixer(q, k, v):
    B, S, W = q.shape
    spec = pl.BlockSpec((1, S, LANES), lambda b, h: (b, 0, h))
    return pl.pallas_call(
        functools.partial(_sb_kernel, n_blocks=S // SB_BLOCK),
        grid=(B, W // LANES),
        in_specs=[spec, spec, spec],
        out_specs=spec,
        out_shape=jax.ShapeDtypeStruct((B, S, W), BF16),
        compiler_params=_params("parallel", "parallel"),
        name="sb_mixer",
    )(q, k, v)


def _alibi_slope(head):
    return float(np.exp2(-8.0 * (head + 1) / DIL_HEADS))


DIL_STEP_BLOCKS = 2


def _dil_kernel(q_ref, kp_ref, kc_ref, vp_ref, vc_ref, o_ref, l_ref, *, group, dil, n_back):
    BQ = DIL_BLOCK
    n = pl.program_id(2)
    row = lax.broadcasted_iota(jnp.int32, (BQ, BQ), 0)
    col = lax.broadcasted_iota(jnp.int32, (BQ, BQ), 1)
    off_c = row - col
    off_p = off_c + BQ
    valid_c = (off_c >= 0) & (off_c <= n_back)
    for sb in range(DIL_STEP_BLOCKS):
        rows = slice(sb * BQ, (sb + 1) * BQ)
        if sb == 0:
            valid_p = (off_p <= n_back) & (n > 0)
            k_prev, v_prev = kp_ref, vp_ref
            prev_rows = slice(0, BQ)
        else:
            valid_p = off_p <= n_back
            k_prev, v_prev = kc_ref, vc_ref
            prev_rows = slice((sb - 1) * BQ, sb * BQ)
        outs, lses = [], []
        for hh in range(DIL_HEADS_PER_GROUP):
            hs = slice(hh * HEAD_DIM, (hh + 1) * HEAD_DIM)
            slope = _alibi_slope(group * DIL_HEADS_PER_GROUP + hh) * dil
            q = q_ref[0, rows, hs] * (HEAD_DIM ** -0.5)
            s_c = lax.dot_general(q, kc_ref[0, rows, hs], NT_DIMS, preferred_element_type=F32)
            s_p = lax.dot_general(q, k_prev[0, prev_rows, hs], NT_DIMS, preferred_element_type=F32)
            s_c = jnp.where(valid_c, s_c - slope * off_c.astype(F32), MASK_VALUE)
            s_p = jnp.where(valid_p, s_p - slope * off_p.astype(F32), MASK_VALUE)
            m = jnp.maximum(jnp.max(s_c, axis=1, keepdims=True), jnp.max(s_p, axis=1, keepdims=True))
            e_c = jnp.where(valid_c, jnp.exp(s_c - m), 0.0)
            e_p = jnp.where(valid_p, jnp.exp(s_p - m), 0.0)
            den = jnp.sum(e_c, axis=1, keepdims=True) + jnp.sum(e_p, axis=1, keepdims=True)
            o = (jnp.dot(e_c.astype(BF16), vc_ref[0, rows, hs], preferred_element_type=F32)
                 + jnp.dot(e_p.astype(BF16), v_prev[0, prev_rows, hs], preferred_element_type=F32))
            outs.append(o / den)
            lses.append(jnp.broadcast_to(m + jnp.log(den), (BQ, HEAD_DIM)))
        o_ref[0, rows, :] = jnp.concatenate(outs, axis=1).astype(o_ref.dtype)
        l_ref[0, rows, :] = jnp.concatenate(lses, axis=1)


def _dil_group(q, k, v, group):
    window, dil = DIL_GROUPS[group]
    B, S, W = q.shape
    L = S // dil
    nb = L // DIL_BLOCK
    view = lambda a: a.reshape(B, L, dil * W)
    sbk = DIL_STEP_BLOCKS
    cur = pl.BlockSpec((1, sbk * DIL_BLOCK, DIL_OUT_W), lambda b, r, n: (b, n, r))
    prev = pl.BlockSpec((1, DIL_BLOCK, DIL_OUT_W), lambda b, r, n: (b, jnp.maximum(sbk * n - 1, 0), r))
    out = cur
    o, lse = pl.pallas_call(
        functools.partial(_dil_kernel, group=group, dil=dil, n_back=window // dil),
        grid=(B, dil, nb // sbk),
        in_specs=[cur, prev, cur, prev, cur],
        out_specs=[out, out],
        out_shape=[jax.ShapeDtypeStruct((B, L, dil * DIL_OUT_W), BF16),
                   jax.ShapeDtypeStruct((B, L, dil * DIL_OUT_W), F32)],
        compiler_params=_params("parallel", "parallel", "parallel"),
        name=f"dil_group{group}",
    )(view(q), view(k), view(k), view(v), view(v))
    return o.reshape(B, S, DIL_OUT_W), lse.reshape(B, S, DIL_OUT_W)


def _mem_attn_kernel(q_ref, k_ref, v_ref, o_ref):
    outs = []
    for hh in range(MEM_HEADS):
        hs = slice(hh * HEAD_DIM, (hh + 1) * HEAD_DIM)
        q = q_ref[0, :, hs] * (HEAD_DIM ** -0.5)
        s = lax.dot_general(q, k_ref[0, :, hs], NT_DIMS, preferred_element_type=F32)
        e = jnp.exp(s - jnp.max(s, axis=1, keepdims=True))
        o = jnp.dot(e.astype(BF16), v_ref[0, :, hs], preferred_element_type=F32)
        outs.append(o / jnp.sum(e, axis=1, keepdims=True))
    o_ref[0] = jnp.concatenate(outs, axis=1).astype(o_ref.dtype)


def _mem_attention(q, mem_k, mem_v, tm=1024):
    B, S, W = q.shape
    M = mem_k.shape[1]
    tm = min(tm, S)
    tok = pl.BlockSpec((1, tm, W), lambda b, s: (b, s, 0))
    kv = pl.BlockSpec((1, M, W), lambda b, s: (b, 0, 0))
    return pl.pallas_call(
        _mem_attn_kernel,
        grid=(B, S // tm),
        in_specs=[tok, kv, kv],
        out_specs=tok,
        out_shape=jax.ShapeDtypeStruct((B, S, W), BF16),
        compiler_params=_params("parallel", "parallel"),
        name="mem_attention",
    )(q, mem_k, mem_v)


MOE_TM = 512
MOE_ALIGN = 16
MOE_R = 2 * MOE_TM + MOE_EXPERTS * MOE_ALIGN
MOE_NB = MOE_R // MOE_ALIGN
MOE_XW = D_MODEL + LANES
MOE_TME = 512
EXPERT_LANE0 = MOE_GROUPS


def _route(h, wr_ref, b_ref, tri_ref):
    tm = h.shape[0]
    h_hi = h.astype(BF16)
    h_lo = (h - h_hi.astype(F32)).astype(BF16)
    lg = jnp.dot(h_hi, wr_ref[...], preferred_element_type=F32)
    logits = (lg[:, :LANES] + lg[:, LANES:]
              + jnp.dot(h_lo, wr_ref[:, :LANES], preferred_element_type=F32) + b_ref[...])
    lane = lax.broadcasted_iota(jnp.int32, logits.shape, 1)
    neg = jnp.float32(-1e30)

    def first_argmax(vals):
        mx = jnp.max(vals, axis=1, keepdims=True)
        return mx, jnp.min(jnp.where(vals == mx, lane, LANES), axis=1, keepdims=True)

    glog = jnp.where(lane < MOE_GROUPS, logits, neg)
    gmax, gidx = first_argmax(glog)
    group_gate = 1.0 / jnp.sum(jnp.where(lane < MOE_GROUPS, jnp.exp(glog - gmax), 0.0),
                               axis=1, keepdims=True)
    lo = MOE_GROUPS + gidx * MOE_EXPERTS_PER_GROUP
    elog = jnp.where((lane >= lo) & (lane < lo + MOE_EXPERTS_PER_GROUP), logits, neg)
    m1, i1 = first_argmax(elog)
    m2, i2 = first_argmax(jnp.where(lane == i1, neg, elog))
    g1 = group_gate / (1.0 + jnp.exp(m2 - m1))
    g2 = group_gate - g1

    chosen = jnp.where((lane == i1) | (lane == i2), 1.0, 0.0)
    earlier = jnp.dot(tri_ref[...], chosen.astype(BF16), preferred_element_type=F32)
    counts = jnp.sum(chosen, axis=0, keepdims=True)
    padded = jnp.floor((counts + (MOE_ALIGN - 1)) * (1.0 / MOE_ALIGN)) * MOE_ALIGN
    ur = lax.broadcasted_iota(jnp.int32, (LANES, LANES), 0)
    uc = lax.broadcasted_iota(jnp.int32, (LANES, LANES), 1)
    before = jnp.where(ur < uc, 1.0, 0.0).astype(BF16)
    sub8 = lax.broadcasted_iota(jnp.int32, (8, LANES), 0)
    padded8 = jnp.broadcast_to(padded, (8, LANES))
    start8 = jnp.dot(padded8.astype(BF16), before, preferred_element_type=F32)
    pos = earlier + start8[0:1]
    r1 = jnp.sum(jnp.where(lane == i1, pos, 0.0), axis=1, keepdims=True)
    r2 = jnp.sum(jnp.where(lane == i2, pos, 0.0), axis=1, keepdims=True)
    tok = jnp.where(lane == 0, r1, jnp.where(lane == 1, r2, jnp.where(lane == 2, g1,
                                                                      jnp.where(lane == 3, g2, 0.0))))
    tile_info = jnp.where(sub8 == 0, padded8, jnp.where(sub8 == 1, start8, 0.0))
    return tok, tok.T[:8], tile_info


def _mix_from_refs(mix_refs):
    if len(mix_refs) == 1:
        return mix_refs[0][...]
    o0, o1, o2, l0, l1, l2 = [r[...] for r in mix_refs]
    m = jnp.maximum(jnp.maximum(l0, l1), l2)
    w0, w1, w2 = jnp.exp(l0 - m), jnp.exp(l1 - m), jnp.exp(l2 - m)
    mix = (w0 * o0.astype(F32) + w1 * o1.astype(F32) + w2 * o2.astype(F32)) / (w0 + w1 + w2)
    return mix.astype(BF16)


def _out_proj_router_kernel(*refs, n_mix):
    mix_refs = refs[:n_mix]
    mem_ref, x_ref, w_ref, g_ref, wr_ref, b_ref, tri_ref = refs[n_mix:n_mix + 7]
    xo_ref, h_ref, tok_ref, tokt_ref, info_ref = refs[n_mix + 7:]
    mix = _mix_from_refs(mix_refs)
    nm = mix.shape[-1]
    x = (x_ref[...] + jnp.dot(mix, w_ref[:nm, :], preferred_element_type=F32)
         + jnp.dot(mem_ref[...], w_ref[nm:, :], preferred_element_type=F32))
    xo_ref[...] = x
    h = _rms(x, g_ref[...])
    h_ref[...] = h.astype(h_ref.dtype)
    tok, tok_t, info = _route(h, wr_ref, b_ref, tri_ref)
    tok_ref[...] = tok
    tokt_ref[0] = tok_t
    info_ref[0] = info


def _out_proj_router(mix_parts, mem_out, x2, w_out, gain, w_router, b_router):
    T, D = x2.shape
    tm = min(MOE_TM, T)
    nt = T // tm
    tok = lambda wd: pl.BlockSpec((tm, wd), lambda i: (i, 0))
    const = lambda a: pl.BlockSpec(a.shape, lambda i: (0,) * a.ndim)
    w_hi = w_router.astype(BF16)
    w_lo = (w_router - w_hi.astype(F32)).astype(BF16)
    wr = jnp.concatenate([w_hi, w_lo], axis=1)
    tri = jnp.asarray(np.tril(np.ones((tm, tm), np.float32), -1), BF16)
    ins = list(mix_parts) + [mem_out, x2]
    consts = [w_out, gain.reshape(1, D), wr, b_router, tri]
    return pl.pallas_call(
        functools.partial(_out_proj_router_kernel, n_mix=len(mix_parts)),
        grid=(nt,),
        in_specs=[tok(a.shape[1]) for a in ins] + [const(a) for a in consts],
        out_specs=[tok(D), tok(D), tok(LANES),
                   pl.BlockSpec((1, 8, tm), lambda i: (i, 0, 0)),
                   pl.BlockSpec((1, 8, LANES), lambda i: (i, 0, 0))],
        out_shape=[jax.ShapeDtypeStruct((T, D), F32), jax.ShapeDtypeStruct((T, D), BF16),
                   jax.ShapeDtypeStruct((T, LANES), F32), jax.ShapeDtypeStruct((nt, 8, tm), F32),
                   jax.ShapeDtypeStruct((nt, 8, LANES), F32)],
        compiler_params=_params("parallel"),
        name="out_proj_router",
    )(*ins, *consts)


def _moe_tables(info, n_ffn_tiles):
    E = MOE_EXPERTS
    tiles = info.shape[0]
    lanes = slice(EXPERT_LANE0, EXPERT_LANE0 + E)
    rows_te = info[:, 0, lanes].astype(jnp.int32)
    start_te = info[:, 1, lanes].astype(jnp.int32)
    end_te = start_te + rows_te
    n_blk = end_te[:, -1] // MOE_ALIGN
    total = rows_te.sum(axis=0)
    cap = -(-total // MOE_TME) * MOE_TME
    cap_end = jnp.cumsum(cap)
    base = cap_end - cap
    run = jnp.cumsum(rows_te, axis=0) - rows_te
    rows = jnp.arange(MOE_NB, dtype=jnp.int32) * MOE_ALIGN
    e_of = jnp.minimum(jnp.sum(rows[None, :, None] >= end_te[:, None, :], axis=-1), E - 1)
    take = lambda a: jnp.take_along_axis(a, e_of, axis=1)
    dst_row = base[e_of] + take(run) + rows[None, :] - take(start_te)
    blk_dst = jnp.where(rows[None, :] < end_te[:, -1:], dst_row // MOE_ALIGN, 0)
    n_used = (cap_end[-1] // MOE_TME).reshape(1)
    tile_row = jnp.arange(n_ffn_tiles, dtype=jnp.int32) * MOE_TME
    tile_expert = jnp.minimum(jnp.sum(tile_row[:, None] >= cap_end[None, :], axis=-1), E - 1)
    all_rows = jnp.arange(n_ffn_tiles * MOE_TME // MOE_ALIGN, dtype=jnp.int32) * MOE_ALIGN
    e_row = jnp.minimum(jnp.sum(all_rows[:, None] >= cap_end[None, :], axis=-1), E - 1)
    unwritten = all_rows >= jnp.where(all_rows < cap_end[-1], (base + total)[e_row], 0)
    zero_blk = jnp.nonzero(unwritten, size=_moe_zero_blocks(tiles, n_ffn_tiles), fill_value=0)[0]
    n_zero = jnp.sum(unwritten).reshape(1)
    i32 = lambda a: a.astype(jnp.int32)
    return (i32(blk_dst.reshape(-1)), i32(n_blk), i32(tile_expert), i32(n_used), i32(zero_blk), i32(n_zero))


def _moe_zero_blocks(tiles, n_ffn_tiles):
    return n_ffn_tiles * MOE_TME // MOE_ALIGN - tiles * 2 * MOE_TM // MOE_ALIGN


def _block_copies(i, dst_ref, nblk_ref, copy):
    n = nblk_ref[i]

    def start(k, _):
        copy(k, pl.multiple_of(dst_ref[i * MOE_NB + k] * MOE_ALIGN, MOE_ALIGN)).start()
        return 0

    def wait(k, _):
        copy(k, pl.multiple_of(dst_ref[i * MOE_NB + k] * MOE_ALIGN, MOE_ALIGN)).wait()
        return 0

    lax.fori_loop(0, n, start, 0)
    lax.fori_loop(0, n, wait, 0)


def _dispatch_kernel(dst_ref, nblk_ref, zblk_ref, nz_ref, h_ref, tok_ref, tokt_ref, xs_ref,
                     sorted_ref, zero_ref, sem):
    i = pl.program_id(0)

    @pl.when(i == pl.num_programs(0) - 1)
    def _():
        zero_ref[...] = jnp.zeros_like(zero_ref)

        def zero_copy(k):
            dst = pl.multiple_of(zblk_ref[k] * MOE_ALIGN, MOE_ALIGN)
            return pltpu.make_async_copy(zero_ref, xs_ref.at[pl.ds(dst, MOE_ALIGN), :], sem)

        def start(k, _):
            zero_copy(k).start()
            return 0

        def wait(k, _):
            zero_copy(k).wait()
            return 0

        def batch(c, _):
            lo = c * MOE_NB
            hi = jnp.minimum(lo + MOE_NB, nz_ref[0])
            lax.fori_loop(lo, hi, start, 0)
            lax.fori_loop(lo, hi, wait, 0)
            return 0

        lax.fori_loop(0, (nz_ref[0] + MOE_NB - 1) // MOE_NB, batch, 0)

    tm, D = h_ref.shape
    R = sorted_ref.shape[0]
    sub = lax.broadcasted_iota(jnp.int32, (R, tm), 0)
    p1 = sub == tokt_ref[0, 0:1, :].astype(jnp.int32)
    p2 = sub == tokt_ref[0, 1:2, :].astype(jnp.int32)
    both = jnp.where(p1 | p2, 1.0, 0.0).astype(BF16)
    sorted_ref[:, :D] = jnp.dot(both, h_ref[...], preferred_element_type=F32).astype(BF16)
    lane = lax.broadcasted_iota(jnp.int32, (tm, LANES), 1)

    def parts(g):
        hi = g.astype(BF16).astype(F32)
        return jnp.where(lane == 0, hi, jnp.where(lane == 1, g - hi, 0.0)).astype(BF16)

    tok = tok_ref[...]
    gates = (jnp.dot(jnp.where(p1, 1.0, 0.0).astype(BF16), parts(tok[:, 2:3]), preferred_element_type=F32)
             + jnp.dot(jnp.where(p2, 1.0, 0.0).astype(BF16), parts(tok[:, 3:4]), preferred_element_type=F32))
    sorted_ref[:, D:] = gates.astype(BF16)

    def copy(k, dst):
        return pltpu.make_async_copy(
            sorted_ref.at[pl.ds(pl.multiple_of(k * MOE_ALIGN, MOE_ALIGN), MOE_ALIGN), :],
            xs_ref.at[pl.ds(dst, MOE_ALIGN), :], sem)

    _block_copies(i, dst_ref, nblk_ref, copy)


def _ffn_kernel(te_ref, nu_ref, x_ref, w1_ref, w3_ref, w2_ref, o_ref):
    D = w1_ref.shape[1]

    @pl.when(pl.program_id(0) < nu_ref[0])
    def _():
        x = x_ref[:, :D]
        gate = x_ref[:, D:D + 1].astype(F32) + x_ref[:, D + 1:D + 2].astype(F32)
        a = jnp.dot(x, w1_ref[0], preferred_element_type=F32)
        b = jnp.dot(x, w3_ref[0], preferred_element_type=F32)
        hidden = (_silu(a) * b).astype(BF16)
        o_ref[...] = (jnp.dot(hidden, w2_ref[0], preferred_element_type=F32) * gate).astype(o_ref.dtype)

    @pl.when(pl.program_id(0) >= nu_ref[0])
    def _():
        o_ref[...] = jnp.zeros_like(o_ref)


def _combine_kernel(dst_ref, nblk_ref, x_ref, tok_ref, fg_ref, ys_ref, o_ref, sorted_ref, sem, *, final):
    i = pl.program_id(0)
    tm = x_ref.shape[0]
    R = sorted_ref.shape[0]

    @pl.when(i == 0)
    def _():
        sorted_ref[...] = jnp.zeros_like(sorted_ref)

    def copy(k, src):
        return pltpu.make_async_copy(
            ys_ref.at[pl.ds(src, MOE_ALIGN), :],
            sorted_ref.at[pl.ds(pl.multiple_of(k * MOE_ALIGN, MOE_ALIGN), MOE_ALIGN), :], sem)

    _block_copies(i, dst_ref, nblk_ref, copy)
    lane = lax.broadcasted_iota(jnp.int32, (tm, R), 1)
    tok = tok_ref[...]
    pick = jnp.where((lane == tok[:, 0:1].astype(jnp.int32)) | (lane == tok[:, 1:2].astype(jnp.int32)),
                     1.0, 0.0).astype(BF16)
    y = x_ref[...] + jnp.dot(pick, sorted_ref[...], preferred_element_type=F32)
    o_ref[...] = _rms(y, fg_ref[...]) if final else y


def _moe_experts(x2, h, tok, tok_t, info, w1, w3, w2, final_gain=None):
    T, D = x2.shape
    E, _, H = w1.shape
    tm = min(MOE_TM, T)
    nt = T // tm
    n_ffn = -(-(nt * MOE_R + E * (MOE_TME - 1)) // MOE_TME)
    blk_dst, n_blk, tile_expert, n_used, zero_blk, n_zero = _moe_tables(info, n_ffn)
    rows = n_ffn * MOE_TME

    xs = pl.pallas_call(
        _dispatch_kernel,
        grid_spec=pltpu.PrefetchScalarGridSpec(
            num_scalar_prefetch=4, grid=(nt,),
            in_specs=[pl.BlockSpec((tm, D), lambda i, *_: (i, 0)),
                      pl.BlockSpec((tm, LANES), lambda i, *_: (i, 0)),
                      pl.BlockSpec((1, 8, tm), lambda i, *_: (i, 0, 0))],
            out_specs=pl.BlockSpec(memory_space=pl.ANY),
            scratch_shapes=[pltpu.VMEM((MOE_R, MOE_XW), BF16), pltpu.VMEM((MOE_ALIGN, MOE_XW), BF16),
                            pltpu.SemaphoreType.DMA]),
        out_shape=jax.ShapeDtypeStruct((rows, MOE_XW), BF16),
        compiler_params=_params("arbitrary"),
        name="moe_dispatch",
    )(blk_dst, n_blk, zero_blk, n_zero, h, tok, tok_t)

    last = lambda n, nu: jnp.minimum(n, nu[0] - 1)
    ys = pl.pallas_call(
        _ffn_kernel,
        grid_spec=pltpu.PrefetchScalarGridSpec(
            num_scalar_prefetch=2, grid=(n_ffn,),
            in_specs=[pl.BlockSpec((MOE_TME, MOE_XW), lambda n, te, nu: (last(n, nu), 0)),
                      pl.BlockSpec((1, D, H), lambda n, te, nu: (te[n], 0, 0)),
                      pl.BlockSpec((1, D, H), lambda n, te, nu: (te[n], 0, 0)),
                      pl.BlockSpec((1, H, D), lambda n, te, nu: (te[n], 0, 0))],
            out_specs=pl.BlockSpec((MOE_TME, D), lambda n, te, nu: (n, 0))),
        out_shape=jax.ShapeDtypeStruct((rows, D), BF16),
        compiler_params=_params("arbitrary"),
        name="moe_ffn",
    )(tile_expert, n_used, xs, w1, w3, w2)

    fg = jnp.ones((1, D), F32) if final_gain is None else final_gain.reshape(1, D).astype(F32)
    return pl.pallas_call(
        functools.partial(_combine_kernel, final=final_gain is not None),
        grid_spec=pltpu.PrefetchScalarGridSpec(
            num_scalar_prefetch=2, grid=(nt,),
            in_specs=[pl.BlockSpec((tm, D), lambda i, *_: (i, 0)),
                      pl.BlockSpec((tm, LANES), lambda i, *_: (i, 0)),
                      pl.BlockSpec((1, D), lambda i, *_: (0, 0)),
                      pl.BlockSpec(memory_space=pl.ANY)],
            out_specs=pl.BlockSpec((tm, D), lambda i, *_: (i, 0)),
            scratch_shapes=[pltpu.VMEM((MOE_R, D), BF16), pltpu.SemaphoreType.DMA]),
        out_shape=jax.ShapeDtypeStruct((T, D), F32),
        compiler_params=_params("arbitrary"),
        name="moe_combine",
    )(blk_dst, n_blk, x2, tok, fg, ys)


def _segments(widths):
    starts = np.cumsum([0] + list(widths[:-1]))
    return [(int(s), int(w)) for s, w in zip(starts, widths)]


def kernel(x, mem, norm1, norm2, mem_norm, final_norm, mem_w_kv, hgrn_w_in, hgrn_onorm, hgrn_lb_logits, hgrn_w_out, sb_w_in, sb_w_out, dil_w_in, dil_w_out, moe_w_group, moe_b_group, moe_w_expert, moe_b_expert, moe_w1, moe_w3, moe_w2):
    B, S, D = x.shape
    depth = norm1.shape[0]
    T = B * S
    x2 = x.reshape(T, D)

    lb_w = jax.nn.softmax(hgrn_lb_logits.astype(F32), axis=0)
    lower_bounds = jnp.maximum(jnp.cumsum(lb_w, axis=0) - lb_w[:1], 0.0)

    M = mem.shape[1]
    w_kv = mem_w_kv.transpose(1, 0, 2).reshape(D, depth * 2 * MEM_W).astype(BF16)
    mem_kv = _norm_proj(mem.reshape(B * M, D), mem_norm, w_kv, _segments([MEM_W] * (2 * depth)),
                        [BF16] * (2 * depth))

    for layer in range(depth):
        kind, j = layer % N_MIXERS, layer // N_MIXERS
        mem_k = mem_kv[2 * layer].reshape(B, M, MEM_W)
        mem_v = mem_kv[2 * layer + 1].reshape(B, M, MEM_W)
        seq = lambda a: a.reshape(B, S, a.shape[-1])
        if kind == 0:
            q, f, i, g, mq = _norm_proj(x2, norm1[layer], hgrn_w_in[j].astype(BF16),
                                        _segments([MIX_W] * 4 + [MEM_W]), [BF16, F32, BF16, BF16, BF16])
            mix = _hgrn_mixer(seq(q), seq(f), seq(i), seq(g), lower_bounds[layer], hgrn_onorm[j])
            mix_parts, w_out = [mix.reshape(T, MIX_W)], hgrn_w_out[j]
        elif kind == 1:
            q, k, v, mq = _norm_proj(x2, norm1[layer], sb_w_in[j].astype(BF16),
                                     _segments([MIX_W] * 3 + [MEM_W]), [BF16] * 4)
            mix = _sb_mixer(seq(q), seq(k), seq(v))
            mix_parts, w_out = [mix.reshape(T, MIX_W)], sb_w_out[j]
        else:
            n_grp = len(DIL_GROUPS)
            *qkv, mq = _norm_proj(x2, norm1[layer], dil_w_in[j].astype(BF16),
                                  _segments([DIL_OUT_W] * (3 * n_grp) + [MEM_W]), [BF16] * (3 * n_grp + 1))
            parts = [_dil_group(seq(qkv[grp]), seq(qkv[n_grp + grp]), seq(qkv[2 * n_grp + grp]), grp)
                     for grp in range(n_grp)]
            mix_parts = ([o.reshape(T, DIL_OUT_W) for o, _ in parts]
                         + [l.reshape(T, DIL_OUT_W) for _, l in parts])
            w_out = dil_w_out[j]
        mem_out = _mem_attention(seq(mq), mem_k, mem_v).reshape(T, MEM_W)

        pad = LANES - MOE_GROUPS - MOE_EXPERTS
        w_router = jnp.pad(jnp.concatenate([moe_w_group[layer], moe_w_expert[layer]], axis=1),
                           ((0, 0), (0, pad))).astype(F32)
        b_router = jnp.pad(jnp.concatenate([moe_b_group[layer], moe_b_expert[layer]]),
                           (0, pad)).astype(F32).reshape(1, LANES)
        x2, h, tok, tok_t, info = _out_proj_router(mix_parts, mem_out, x2, w_out.astype(BF16),
                                                   norm2[layer], w_router, b_router)
        x2 = _moe_experts(x2, h, tok, tok_t, info, moe_w1[layer].astype(BF16),
                          moe_w3[layer].astype(BF16), moe_w2[layer].astype(BF16),
                          final_gain=final_norm if layer == depth - 1 else None)

    return x2.reshape(B, S, D)
```

```python
import functools

import numpy as np
import jax
import jax.numpy as jnp
from jax import lax
from jax.experimental import pallas as pl
from jax.experimental.pallas import tpu as pltpu

D_MODEL = 1024
HEAD_DIM = 64
MEM_HEADS = 4
MEM_W = MEM_HEADS * HEAD_DIM
MIX_W = D_MODEL - MEM_W
N_MIXERS = 3

HGRN_HEAD_DIM = 128
HGRN_HEADS = MIX_W // HGRN_HEAD_DIM
LB_EPS = 1e-30

DIL_GROUPS = ((128, 1), (512, 4), (2048, 16))
DIL_HEADS = MIX_W // HEAD_DIM
DIL_HEADS_PER_GROUP = DIL_HEADS // len(DIL_GROUPS)
DIL_OUT_W = DIL_HEADS_PER_GROUP * HEAD_DIM
DIL_BLOCK = 128
MASK_VALUE = -1e30

MOE_GROUPS = 4
MOE_EXPERTS_PER_GROUP = 4
MOE_EXPERTS = MOE_GROUPS * MOE_EXPERTS_PER_GROUP
MOE_HIDDEN = 512
RMS_EPS = 1e-6

LANES = 128
VMEM_LIMIT = 48 * 1024 * 1024

F32_EXP_UNDERFLOW = -104.0

BF16 = jnp.bfloat16
F32 = jnp.float32
NT_DIMS = (((1,), (1,)), ((), ()))


def _params(*semantics):
    return pltpu.CompilerParams(dimension_semantics=semantics, vmem_limit_bytes=VMEM_LIMIT)


def _log_sigmoid(z):
    return jnp.minimum(z, 0.0) - jnp.log(1.0 + jnp.exp(-jnp.abs(z)))


def _silu(z):
    return z / (1.0 + jnp.exp(-z))


def _rms(x, gain):
    return x * lax.rsqrt(jnp.mean(x * x, axis=-1, keepdims=True) + RMS_EPS) * gain


def _norm_proj_kernel(x_ref, g_ref, w_ref, *out_refs, segs):
    h = _rms(x_ref[...], g_ref[...]).astype(BF16)
    for (start, width), o_ref in zip(segs, out_refs):
        o_ref[...] = jnp.dot(h, w_ref[:, start:start + width],
                             preferred_element_type=F32).astype(o_ref.dtype)


def _norm_proj(x2, gain, w, segs, dtypes, tm=512):
    T, D = x2.shape
    N = w.shape[1]
    tm = min(tm, T)
    return pl.pallas_call(
        functools.partial(_norm_proj_kernel, segs=tuple(segs)),
        grid=(T // tm,),
        in_specs=[pl.BlockSpec((tm, D), lambda i: (i, 0)),
                  pl.BlockSpec((1, D), lambda i: (0, 0)),
                  pl.BlockSpec((D, N), lambda i: (0, 0))],
        out_specs=[pl.BlockSpec((tm, wd), lambda i: (i, 0)) for _, wd in segs],
        out_shape=[jax.ShapeDtypeStruct((T, wd), dt) for (_, wd), dt in zip(segs, dtypes)],
        compiler_params=_params("parallel"),
        name="norm_proj",
    )(x2, gain.reshape(1, D), w)


HGRN_C = 128
HGRN_LEVELS = 7


def _hgrn_level_matrix():
    C = HGRN_C
    m = np.zeros((HGRN_LEVELS + 1, C, C), np.float32)
    j = np.arange(C)
    for lvl in range(HGRN_LEVELS):
        bs = C >> lvl
        half = bs // 2
        for t in range(C):
            r = (t // bs) * bs + half - 1
            if t & half:
                m[lvl, t] = (j > r) & (j <= t)
            else:
                m[lvl, t] = (j > t) & (j <= r)
    m[HGRN_LEVELS] = j[None, :] <= j[:, None]
    return m.reshape((HGRN_LEVELS + 1) * C, C)


def _hgrn_kernel(q_ref, f_ref, i_ref, g_ref, lb_ref, gain_ref, m_ref, o_ref, st_ref, *, n_chunks):
    C = HGRN_C

    @pl.when(pl.program_id(2) == 0)
    def _():
        st_ref[...] = jnp.zeros_like(st_ref)

    row = lax.broadcasted_iota(jnp.int32, (C, C), 0)
    col = lax.broadcasted_iota(jnp.int32, (C, C), 1)
    rcol = lax.broadcasted_iota(jnp.int32, (C, 1), 0)
    log_lb = jnp.log(lb_ref[...] + LB_EPS)
    gain = gain_ref[...]

    chunks = []
    for c in range(n_chunks):
        sl = slice(c * C, (c + 1) * C)
        z = f_ref[0, sl, :]
        a1 = _log_sigmoid(z)
        a2 = log_lb + (a1 - z)
        log_f = jnp.maximum(a1, a2) + jnp.log(1.0 + jnp.exp(-jnp.abs(a1 - a2)))
        log_f = jnp.minimum(log_f, 0.0)
        k = 1.0 - jnp.exp(log_f)
        q = _silu(q_ref[0, sl, :].astype(F32)) * (HGRN_HEAD_DIM ** -0.5)
        g_hi = log_f.astype(BF16)
        g_lo = (log_f - g_hi.astype(F32)).astype(BF16)
        e_all = jnp.dot(m_ref[...], jnp.concatenate([g_hi, g_lo], axis=1),
                        preferred_element_type=F32)
        e_all = e_all[:, :HGRN_HEAD_DIM] + e_all[:, HGRN_HEAD_DIM:]
        chunks.append(dict(sl=sl, k=k, q=q, v=i_ref[0, sl, :], e_all=e_all))

    for ch in chunks:
        q, k, e_all = ch["q"], ch["k"], ch["e_all"]
        pairs = [(q.astype(BF16), k.astype(BF16))]
        for lvl in range(HGRN_LEVELS):
            half = (C >> lvl) // 2
            dec = jnp.exp(e_all[lvl * C:(lvl + 1) * C])
            upper = (rcol & half) != 0
            pairs.append((jnp.where(upper, q * dec, 0.0).astype(BF16),
                          jnp.where(upper, 0.0, k * dec).astype(BF16)))
        ch["pairs"] = pairs
    for ch in chunks:
        ch["p"] = [lax.dot_general(qt, kt, NT_DIMS, preferred_element_type=F32) for qt, kt in ch["pairs"]]
    for ch in chunks:
        attn = jnp.where(row == col, ch["p"][0], 0.0)
        for lvl in range(HGRN_LEVELS):
            attn = attn + jnp.where((row ^ col) < (C >> lvl), ch["p"][lvl + 1], 0.0)
        b = ch["e_all"][HGRN_LEVELS * C:]
        b_end = b[C - 1:C, :]
        ch["attn"] = attn.astype(BF16)
        ch["q_dec"] = (ch["q"] * jnp.exp(b)).astype(BF16)
        ch["k_end"] = (ch["k"] * jnp.exp(b_end - b)).astype(BF16)
        ch["v_t"] = ch["v"].astype(F32).T.astype(BF16)
        ch["keep"] = jnp.exp(b_end)
    for ch in chunks:
        ch["o"] = jnp.dot(ch["attn"], ch["v"], preferred_element_type=F32)
        ch["kv"] = jnp.dot(ch["v_t"], ch["k_end"], preferred_element_type=F32)

    st = st_ref[...]
    for ch in chunks:
        ch["o"] = ch["o"] + lax.dot_general(ch["q_dec"], st.astype(BF16), NT_DIMS,
                                            preferred_element_type=F32)
        st = st * ch["keep"] + ch["kv"]
    st_ref[...] = st

    for ch in chunks:
        gate = _silu(g_ref[0, ch["sl"], :].astype(F32))
        o_ref[0, ch["sl"], :] = (_rms(ch["o"], gain) * gate).astype(o_ref.dtype)


def _hgrn_mixer(q, f, i, g, lower_bound, out_gain, ts=512):
    B, S, W = q.shape
    dh = HGRN_HEAD_DIM
    ts = min(ts, S)
    seq = pl.BlockSpec((1, ts, dh), lambda b, h, s: (b, s, h))
    m = jnp.asarray(_hgrn_level_matrix(), BF16)
    return pl.pallas_call(
        functools.partial(_hgrn_kernel, n_chunks=ts // HGRN_C),
        grid=(B, W // dh, S // ts),
        in_specs=[seq, seq, seq, seq,
                  pl.BlockSpec((1, dh), lambda b, h, s: (0, h)),
                  pl.BlockSpec((1, dh), lambda b, h, s: (0, 0)),
                  pl.BlockSpec(m.shape, lambda b, h, s: (0, 0))],
        out_specs=seq,
        out_shape=jax.ShapeDtypeStruct((B, S, W), BF16),
        scratch_shapes=[pltpu.VMEM((dh, dh), F32)],
        compiler_params=_params("parallel", "parallel", "arbitrary"),
        name="hgrn_mixer",
    )(q, f, i, g, lower_bound.reshape(1, W), out_gain.reshape(1, dh), m)


SB_BLOCK = 256
SB_HALF = LANES


def _sb_kernel(q_ref, k_ref, v_ref, o_ref, *, n_blocks):
    BQ, HB = SB_BLOCK, SB_HALF
    heads = [slice(h * HEAD_DIM, (h + 1) * HEAD_DIM) for h in range(LANES // HEAD_DIM)]
    row = lax.broadcasted_iota(jnp.int32, (BQ, BQ), 0)
    col = lax.broadcasted_iota(jnp.int32, (BQ, BQ), 1)
    tr = lax.broadcasted_iota(jnp.int32, (HB, 2 * HB), 0)
    tc = lax.broadcasted_iota(jnp.int32, (HB, 2 * HB), 1)
    suffix = jnp.where((tc >= HB) | (tr > tc), 1.0, 0.0).astype(BF16)

    def step(qs, k0, accs, cs, diagonal):
        earlier = col < row
        zs = [lax.dot_general(q, k_ref[0, pl.ds(k0, BQ), hs], NT_DIMS, preferred_element_type=F32)
              for q, hs in zip(qs, heads)]
        lss, lhss = [], []
        for z in zs:
            ls = _log_sigmoid(z)
            lk = ls - z
            if diagonal:
                lk = jnp.where(earlier, lk, 0.0)
            lk_hi = lk.astype(BF16)
            lk_lo = (lk - lk_hi.astype(F32)).astype(BF16)
            lss.append(ls)
            lhss.append(jnp.concatenate([lk_hi[:, HB:], lk_lo[:, HB:], lk_hi[:, :HB], lk_lo[:, :HB]], axis=0))
        ress = [jnp.dot(lhs, suffix, preferred_element_type=F32) for lhs in lhss]
        weights, new_cs = [], []
        for ls, res, c in zip(lss, ress, cs):
            res_r = res[:BQ] + res[BQ:2 * BQ]
            res_l = res[2 * BQ:3 * BQ] + res[3 * BQ:]
            stick_r = res_r[:, :HB] + c
            c = c + res_r[:, HB:]
            stick_l = res_l[:, :HB] + c
            new_cs.append(c + res_l[:, HB:])
            a = jnp.exp(ls + jnp.concatenate([stick_l, stick_r], axis=1))
            if diagonal:
                a = jnp.where(earlier, a, 0.0)
            weights.append(a.astype(BF16))
        accs = [acc + jnp.dot(a, v_ref[0, pl.ds(k0, BQ), hs], preferred_element_type=F32)
                for acc, a, hs in zip(accs, weights, heads)]
        return accs, new_cs

    def keep_going(c0, c1):
        return (jnp.max(jnp.maximum(c0, c1)) > F32_EXP_UNDERFLOW).astype(jnp.int32)

    def q_block(qb, _):
        q0 = pl.multiple_of(qb * BQ, BQ)
        qs = [q_ref[0, pl.ds(q0, BQ), hs] * (HEAD_DIM ** -0.5) for hs in heads]
        (acc0, acc1), (c0, c1) = step(qs, q0, [jnp.zeros((BQ, HEAD_DIM), F32)] * 2,
                                      [jnp.zeros((BQ, HB), F32)] * 2, True)

        def cond(state):
            return jnp.logical_and(state[0] <= qb, state[-1] > 0)

        def body(state):
            j, acc0, acc1, c0, c1, _ = state
            k0 = pl.multiple_of((qb - j) * BQ, BQ)
            (acc0, acc1), (c0, c1) = step(qs, k0, [acc0, acc1], [c0, c1], False)
            return j + 1, acc0, acc1, c0, c1, keep_going(c0, c1)

        state = lax.while_loop(cond, body, (jnp.int32(1), acc0, acc1, c0, c1, keep_going(c0, c1)))
        o_ref[0, pl.ds(q0, BQ), :] = jnp.concatenate(state[1:3], axis=1).astype(o_ref.dtype)
        return 0

    lax.fori_loop(0, n_blocks, q_block, 0)


def _sb_mixer(q, k, v):
    B, S, W = q.shape
    spec = pl.BlockSpec((1, S, LANES), lambda b, h: (b, 0, h))
    return pl.pallas_call(
        functools.partial(_sb_kernel, n_blocks=S // SB_BLOCK),
        grid=(B, W // LANES),
        in_specs=[spec, spec, spec],
        out_specs=spec,
        out_shape=jax.ShapeDtypeStruct((B, S, W), BF16),
        compiler_params=_params("parallel", "parallel"),
        name="sb_mixer",
    )(q, k, v)


def _alibi_slope(head):
    return float(np.exp2(-8.0 * (head + 1) / DIL_HEADS))


DIL_STEP_BLOCKS = 2


def _dil_kernel(q_ref, kp_ref, kc_ref, vp_ref, vc_ref, o_ref, l_ref, *, group, dil, n_back):
    BQ = DIL_BLOCK
    n = pl.program_id(2)
    row = lax.broadcasted_iota(jnp.int32, (BQ, BQ), 0)
    col = lax.broadcasted_iota(jnp.int32, (BQ, BQ), 1)
    off_c = row - col
    off_p = off_c + BQ
    valid_c = (off_c >= 0) & (off_c <= n_back)
    W = q_ref.shape[-1]
    head_of = lax.broadcasted_iota(jnp.int32, (BQ, W), 1) // HEAD_DIM
    ones = jnp.ones((BQ, LANES), BF16)
    items = []
    for sb in range(DIL_STEP_BLOCKS):
        rows = slice(sb * BQ, (sb + 1) * BQ)
        if sb == 0:
            valid_p = (off_p <= n_back) & (n > 0)
            k_prev, v_prev = kp_ref, vp_ref
            prev_rows = slice(0, BQ)
        else:
            valid_p = off_p <= n_back
            k_prev, v_prev = kc_ref, vc_ref
            prev_rows = slice((sb - 1) * BQ, sb * BQ)
        q_all = q_ref[0, rows, :] * (HEAD_DIM ** -0.5)
        blk = dict(rows=rows, valid_p=valid_p, q=q_all, k_c=kc_ref[0, rows, :], k_p=k_prev[0, prev_rows, :],
                   v_c=vc_ref[0, rows, :], v_p=v_prev[0, prev_rows, :])
        items += [(blk, hh) for hh in range(DIL_HEADS_PER_GROUP)]

    scores = []
    for blk, hh in items:
        q = jnp.where(head_of == hh, blk["q"], jnp.zeros_like(blk["q"]))
        scores.append((lax.dot_general(q, blk["k_c"], NT_DIMS, preferred_element_type=F32),
                       lax.dot_general(q, blk["k_p"], NT_DIMS, preferred_element_type=F32)))
    masked = []
    for (blk, hh), (s_c, s_p) in zip(items, scores):
        slope = _alibi_slope(group * DIL_HEADS_PER_GROUP + hh) * dil
        s_c = jnp.where(valid_c, s_c - slope * off_c.astype(F32), MASK_VALUE)
        s_p = jnp.where(blk["valid_p"], s_p - slope * off_p.astype(F32), MASK_VALUE)
        masked.append((s_c, s_p, jnp.max(jnp.maximum(s_c, s_p), axis=1, keepdims=True)))
    weights = []
    for (blk, hh), (s_c, s_p, m) in zip(items, masked):
        weights.append((jnp.where(valid_c, jnp.exp(s_c - m), 0.0).astype(BF16),
                        jnp.where(blk["valid_p"], jnp.exp(s_p - m), 0.0).astype(BF16)))
    sums = []
    for (blk, hh), (e_c, e_p) in zip(items, weights):
        o = (jnp.dot(e_c, blk["v_c"], preferred_element_type=F32)
             + jnp.dot(e_p, blk["v_p"], preferred_element_type=F32))
        den = (jnp.dot(e_c, ones, preferred_element_type=F32)
               + jnp.dot(e_p, ones, preferred_element_type=F32))
        sums.append((o, jnp.concatenate([den] * (W // LANES), axis=1)))
    for sb in range(DIL_STEP_BLOCKS):
        out = jnp.zeros((BQ, W), F32)
        lse = jnp.zeros((BQ, W), F32)
        for idx in range(sb * DIL_HEADS_PER_GROUP, (sb + 1) * DIL_HEADS_PER_GROUP):
            (blk, hh), (o, den), m = items[idx], sums[idx], masked[idx][2]
            out = jnp.where(head_of == hh, o / den, out)
            lse = jnp.where(head_of == hh, m + jnp.log(den), lse)
        o_ref[0, blk["rows"], :] = out.astype(o_ref.dtype)
        l_ref[0, blk["rows"], :] = lse


def _dil_group(q, k, v, group):
    window, dil = DIL_GROUPS[group]
    B, S, W = q.shape
    L = S // dil
    nb = L // DIL_BLOCK
    view = lambda a: a.reshape(B, L, dil * W)
    sbk = DIL_STEP_BLOCKS
    cur = pl.BlockSpec((1, sbk * DIL_BLOCK, DIL_OUT_W), lambda b, r, n: (b, n, r))
    prev = pl.BlockSpec((1, DIL_BLOCK, DIL_OUT_W), lambda b, r, n: (b, jnp.maximum(sbk * n - 1, 0), r))
    out = cur
    o, lse = pl.pallas_call(
        functools.partial(_dil_kernel, group=group, dil=dil, n_back=window // dil),
        grid=(B, dil, nb // sbk),
        in_specs=[cur, prev, cur, prev, cur],
        out_specs=[out, out],
        out_shape=[jax.ShapeDtypeStruct((B, L, dil * DIL_OUT_W), BF16),
                   jax.ShapeDtypeStruct((B, L, dil * DIL_OUT_W), F32)],
        compiler_params=_params("parallel", "parallel", "parallel"),
        name=f"dil_group{group}",
    )(view(q), view(k), view(k), view(v), view(v))
    return o.reshape(B, S, DIL_OUT_W), lse.reshape(B, S, DIL_OUT_W)


def _mem_attn_kernel(q_ref, k_ref, v_ref, o_ref):
    heads = [slice(hh * HEAD_DIM, (hh + 1) * HEAD_DIM) for hh in range(MEM_HEADS)]
    ss = [lax.dot_general(q_ref[0, :, hs] * (HEAD_DIM ** -0.5), k_ref[0, :, hs], NT_DIMS,
                          preferred_element_type=F32) for hs in heads]
    es = [jnp.exp(s - jnp.max(s, axis=1, keepdims=True)) for s in ss]
    os_ = [jnp.dot(e.astype(BF16), v_ref[0, :, hs], preferred_element_type=F32) for e, hs in zip(es, heads)]
    outs = [o / jnp.sum(e, axis=1, keepdims=True) for o, e in zip(os_, es)]
    o_ref[0] = jnp.concatenate(outs, axis=1).astype(o_ref.dtype)


def _mem_attention(q, mem_k, mem_v, tm=1024):
    B, S, W = q.shape
    M = mem_k.shape[1]
    tm = min(tm, S)
    tok = pl.BlockSpec((1, tm, W), lambda b, s: (b, s, 0))
    kv = pl.BlockSpec((1, M, W), lambda b, s: (b, 0, 0))
    return pl.pallas_call(
        _mem_attn_kernel,
        grid=(B, S // tm),
        in_specs=[tok, kv, kv],
        out_specs=tok,
        out_shape=jax.ShapeDtypeStruct((B, S, W), BF16),
        compiler_params=_params("parallel", "parallel"),
        name="mem_attention",
    )(q, mem_k, mem_v)


MOE_TM = 512
MOE_ALIGN = 16
MOE_R = 2 * MOE_TM + MOE_EXPERTS * MOE_ALIGN
MOE_NB = MOE_R // MOE_ALIGN
MOE_XW = D_MODEL + LANES
MOE_TME = 512
EXPERT_LANE0 = MOE_GROUPS


def _route(h, wr_ref, b_ref, tri_ref):
    tm = h.shape[0]
    h_hi = h.astype(BF16)
    h_lo = (h - h_hi.astype(F32)).astype(BF16)
    lg = jnp.dot(h_hi, wr_ref[...], preferred_element_type=F32)
    logits = (lg[:, :LANES] + lg[:, LANES:]
              + jnp.dot(h_lo, wr_ref[:, :LANES], preferred_element_type=F32) + b_ref[...])
    lane = lax.broadcasted_iota(jnp.int32, logits.shape, 1)
    neg = jnp.float32(-1e30)

    def first_argmax(vals):
        mx = jnp.max(vals, axis=1, keepdims=True)
        return mx, jnp.min(jnp.where(vals == mx, lane, LANES), axis=1, keepdims=True)

    glog = jnp.where(lane < MOE_GROUPS, logits, neg)
    gmax, gidx = first_argmax(glog)
    group_gate = 1.0 / jnp.sum(jnp.where(lane < MOE_GROUPS, jnp.exp(glog - gmax), 0.0),
                               axis=1, keepdims=True)
    lo = MOE_GROUPS + gidx * MOE_EXPERTS_PER_GROUP
    elog = jnp.where((lane >= lo) & (lane < lo + MOE_EXPERTS_PER_GROUP), logits, neg)
    m1, i1 = first_argmax(elog)
    m2, i2 = first_argmax(jnp.where(lane == i1, neg, elog))
    g1 = group_gate / (1.0 + jnp.exp(m2 - m1))
    g2 = group_gate - g1

    chosen = jnp.where((lane == i1) | (lane == i2), 1.0, 0.0)
    earlier = jnp.dot(tri_ref[...], chosen.astype(BF16), preferred_element_type=F32)
    counts = jnp.sum(chosen, axis=0, keepdims=True)
    padded = jnp.floor((counts + (MOE_ALIGN - 1)) * (1.0 / MOE_ALIGN)) * MOE_ALIGN
    ur = lax.broadcasted_iota(jnp.int32, (LANES, LANES), 0)
    uc = lax.broadcasted_iota(jnp.int32, (LANES, LANES), 1)
    before = jnp.where(ur < uc, 1.0, 0.0).astype(BF16)
    sub8 = lax.broadcasted_iota(jnp.int32, (8, LANES), 0)
    padded8 = jnp.broadcast_to(padded, (8, LANES))
    start8 = jnp.dot(padded8.astype(BF16), before, preferred_element_type=F32)
    pos = earlier + start8[0:1]
    r1 = jnp.sum(jnp.where(lane == i1, pos, 0.0), axis=1, keepdims=True)
    r2 = jnp.sum(jnp.where(lane == i2, pos, 0.0), axis=1, keepdims=True)
    tok = jnp.where(lane == 0, r1, jnp.where(lane == 1, r2, jnp.where(lane == 2, g1,
                                                                      jnp.where(lane == 3, g2, 0.0))))
    tile_info = jnp.where(sub8 == 0, padded8, jnp.where(sub8 == 1, start8, 0.0))
    return tok, tok.T[:8], tile_info


def _mix_from_refs(mix_refs):
    if len(mix_refs) == 1:
        return mix_refs[0][...]
    o0, o1, o2, l0, l1, l2 = [r[...] for r in mix_refs]
    m = jnp.maximum(jnp.maximum(l0, l1), l2)
    w0, w1, w2 = jnp.exp(l0 - m), jnp.exp(l1 - m), jnp.exp(l2 - m)
    mix = (w0 * o0.astype(F32) + w1 * o1.astype(F32) + w2 * o2.astype(F32)) / (w0 + w1 + w2)
    return mix.astype(BF16)


def _out_proj_router_kernel(*refs, n_mix):
    mix_refs = refs[:n_mix]
    mem_ref, x_ref, w_ref, g_ref, wr_ref, b_ref, tri_ref = refs[n_mix:n_mix + 7]
    xo_ref, h_ref, tok_ref, tokt_ref, info_ref = refs[n_mix + 7:]
    mix = _mix_from_refs(mix_refs)
    nm = mix.shape[-1]
    x = (x_ref[...] + jnp.dot(mix, w_ref[:nm, :], preferred_element_type=F32)
         + jnp.dot(mem_ref[...], w_ref[nm:, :], preferred_element_type=F32))
    xo_ref[...] = x
    h = _rms(x, g_ref[...])
    h_ref[...] = h.astype(h_ref.dtype)
    tok, tok_t, info = _route(h, wr_ref, b_ref, tri_ref)
    tok_ref[...] = tok
    tokt_ref[0] = tok_t
    info_ref[0] = info


def _out_proj_router(mix_parts, mem_out, x2, w_out, gain, w_router, b_router):
    T, D = x2.shape
    tm = min(MOE_TM, T)
    nt = T // tm
    tok = lambda wd: pl.BlockSpec((tm, wd), lambda i: (i, 0))
    const = lambda a: pl.BlockSpec(a.shape, lambda i: (0,) * a.ndim)
    w_hi = w_router.astype(BF16)
    w_lo = (w_router - w_hi.astype(F32)).astype(BF16)
    wr = jnp.concatenate([w_hi, w_lo], axis=1)
    tri = jnp.asarray(np.tril(np.ones((tm, tm), np.float32), -1), BF16)
    ins = list(mix_parts) + [mem_out, x2]
    consts = [w_out, gain.reshape(1, D), wr, b_router, tri]
    return pl.pallas_call(
        functools.partial(_out_proj_router_kernel, n_mix=len(mix_parts)),
        grid=(nt,),
        in_specs=[tok(a.shape[1]) for a in ins] + [const(a) for a in consts],
        out_specs=[tok(D), tok(D), tok(LANES),
                   pl.BlockSpec((1, 8, tm), lambda i: (i, 0, 0)),
                   pl.BlockSpec((1, 8, LANES), lambda i: (i, 0, 0))],
        out_shape=[jax.ShapeDtypeStruct((T, D), F32), jax.ShapeDtypeStruct((T, D), BF16),
                   jax.ShapeDtypeStruct((T, LANES), F32), jax.ShapeDtypeStruct((nt, 8, tm), F32),
                   jax.ShapeDtypeStruct((nt, 8, LANES), F32)],
        compiler_params=_params("parallel"),
        name="out_proj_router",
    )(*ins, *consts)


def _moe_tables(info, n_ffn_tiles):
    E = MOE_EXPERTS
    lanes = slice(EXPERT_LANE0, EXPERT_LANE0 + E)
    rows_te = info[:, 0, lanes].astype(jnp.int32)
    start_te = info[:, 1, lanes].astype(jnp.int32)
    end_te = start_te + rows_te
    n_blk = end_te[:, -1] // MOE_ALIGN
    total = rows_te.sum(axis=0)
    cap = -(-total // MOE_TME) * MOE_TME
    cap_end = jnp.cumsum(cap)
    base = cap_end - cap
    run = jnp.cumsum(rows_te, axis=0) - rows_te
    rows = jnp.arange(MOE_NB, dtype=jnp.int32) * MOE_ALIGN
    e_of = jnp.sum(rows[None, :, None] >= end_te[:, None, :], axis=-1)
    in_e = e_of[:, :, None] == jnp.arange(E, dtype=jnp.int32)
    shift = base[None, :] + run - start_te
    dst_row = rows[None, :] + jnp.sum(jnp.where(in_e, shift[:, None, :], 0), axis=-1)
    blk_dst = dst_row // MOE_ALIGN
    n_used = (cap_end[-1] // MOE_TME).reshape(1)
    tile_row = jnp.arange(n_ffn_tiles, dtype=jnp.int32) * MOE_TME
    tile_expert = jnp.minimum(jnp.sum(tile_row[:, None] >= cap_end[None, :], axis=-1), E - 1)
    zero_start = jnp.concatenate([base + total, cap_end[-1:]]) // MOE_ALIGN
    zero_count = jnp.concatenate([cap - total, n_ffn_tiles * MOE_TME - cap_end[-1:]]) // MOE_ALIGN
    i32 = lambda a: a.astype(jnp.int32)
    return (i32(blk_dst.reshape(-1)), i32(n_blk), i32(tile_expert), i32(n_used),
            i32(zero_start), i32(zero_count))


def _tile_copies(copy, t, slot, nblk_ref):
    def run(op):
        def body(k, _):
            op(copy(t, slot, k))
            return 0
        lax.fori_loop(0, nblk_ref[t], body, 0)
    return (lambda: run(lambda c: c.start())), (lambda: run(lambda c: c.wait()))


def _dispatch_kernel(dst_ref, nblk_ref, zs_ref, zc_ref, h_ref, tok_ref, tokt_ref, xs_ref,
                     sorted_ref, zero_ref, sems):
    i = pl.program_id(0)
    last = pl.num_programs(0) - 1
    slot = i % 2

    def copy(t, s, k):
        dst = pl.multiple_of(dst_ref[t * MOE_NB + k] * MOE_ALIGN, MOE_ALIGN)
        return pltpu.make_async_copy(
            sorted_ref.at[s, pl.ds(pl.multiple_of(k * MOE_ALIGN, MOE_ALIGN), MOE_ALIGN), :],
            xs_ref.at[pl.ds(dst, MOE_ALIGN), :], sems.at[s])

    @pl.when(i >= 2)
    def _():
        _tile_copies(copy, i - 2, slot, nblk_ref)[1]()

    tm, D = h_ref.shape
    R = sorted_ref.shape[1]
    sub = lax.broadcasted_iota(jnp.int32, (R, tm), 0)
    p1 = sub == tokt_ref[0, 0:1, :].astype(jnp.int32)
    p2 = sub == tokt_ref[0, 1:2, :].astype(jnp.int32)
    both = jnp.where(p1 | p2, 1.0, 0.0).astype(BF16)
    sorted_ref[slot, :, :D] = jnp.dot(both, h_ref[...], preferred_element_type=F32).astype(BF16)
    lane = lax.broadcasted_iota(jnp.int32, (tm, LANES), 1)

    def parts(g):
        hi = g.astype(BF16).astype(F32)
        return jnp.where(lane == 0, hi, jnp.where(lane == 1, g - hi, 0.0)).astype(BF16)

    tok = tok_ref[...]
    gates = (jnp.dot(jnp.where(p1, 1.0, 0.0).astype(BF16), parts(tok[:, 2:3]), preferred_element_type=F32)
             + jnp.dot(jnp.where(p2, 1.0, 0.0).astype(BF16), parts(tok[:, 3:4]), preferred_element_type=F32))
    sorted_ref[slot, :, D:] = gates.astype(BF16)
    _tile_copies(copy, i, slot, nblk_ref)[0]()

    @pl.when(i == last)
    def _():
        @pl.when(i >= 1)
        def _():
            _tile_copies(copy, i - 1, 1 - slot, nblk_ref)[1]()

        _tile_copies(copy, i, slot, nblk_ref)[1]()
        zero_ref[...] = jnp.zeros_like(zero_ref)

        def zero_range(r, _):
            def zero_copy(k):
                dst = pl.multiple_of((zs_ref[r] + k) * MOE_ALIGN, MOE_ALIGN)
                return pltpu.make_async_copy(zero_ref, xs_ref.at[pl.ds(dst, MOE_ALIGN), :], sems.at[0])

            def start(k, _):
                zero_copy(k).start()
                return 0

            def wait(k, _):
                zero_copy(k).wait()
                return 0

            def batch(c, _):
                lo = c * MOE_NB
                hi = jnp.minimum(lo + MOE_NB, zc_ref[r])
                lax.fori_loop(lo, hi, start, 0)
                lax.fori_loop(lo, hi, wait, 0)
                return 0

            lax.fori_loop(0, (zc_ref[r] + MOE_NB - 1) // MOE_NB, batch, 0)
            return 0

        lax.fori_loop(0, MOE_EXPERTS + 1, zero_range, 0)


def _ffn_kernel(te_ref, nu_ref, x_ref, w1_ref, w3_ref, w2_ref, o_ref):
    D = w1_ref.shape[1]

    @pl.when(pl.program_id(0) < nu_ref[0])
    def _():
        x = x_ref[:, :D]
        gate = x_ref[:, D:D + 1].astype(F32) + x_ref[:, D + 1:D + 2].astype(F32)
        a = jnp.dot(x, w1_ref[0], preferred_element_type=F32)
        b = jnp.dot(x, w3_ref[0], preferred_element_type=F32)
        hidden = (_silu(a) * b).astype(BF16)
        o_ref[...] = (jnp.dot(hidden, w2_ref[0], preferred_element_type=F32) * gate).astype(o_ref.dtype)

    @pl.when(pl.program_id(0) >= nu_ref[0])
    def _():
        o_ref[...] = jnp.zeros_like(o_ref)


def _combine_kernel(dst_ref, nblk_ref, x_ref, tok_ref, fg_ref, ys_ref, o_ref, sorted_ref, sems, *, final):
    i = pl.program_id(0)
    slot = i % 2
    tm = x_ref.shape[0]
    R = sorted_ref.shape[1]

    def copy(t, s, k):
        src = pl.multiple_of(dst_ref[t * MOE_NB + k] * MOE_ALIGN, MOE_ALIGN)
        return pltpu.make_async_copy(
            ys_ref.at[pl.ds(src, MOE_ALIGN), :],
            sorted_ref.at[s, pl.ds(pl.multiple_of(k * MOE_ALIGN, MOE_ALIGN), MOE_ALIGN), :], sems.at[s])

    @pl.when(i == 0)
    def _():
        sorted_ref[...] = jnp.zeros_like(sorted_ref)
        _tile_copies(copy, i, slot, nblk_ref)[0]()

    @pl.when(i + 1 < pl.num_programs(0))
    def _():
        _tile_copies(copy, i + 1, 1 - slot, nblk_ref)[0]()

    _tile_copies(copy, i, slot, nblk_ref)[1]()
    lane = lax.broadcasted_iota(jnp.int32, (tm, R), 1)
    tok = tok_ref[...]
    pick = jnp.where((lane == tok[:, 0:1].astype(jnp.int32)) | (lane == tok[:, 1:2].astype(jnp.int32)),
                     1.0, 0.0).astype(BF16)
    y = x_ref[...] + jnp.dot(pick, sorted_ref[slot], preferred_element_type=F32)
    o_ref[...] = _rms(y, fg_ref[...]) if final else y


def _moe_experts(x2, h, tok, tok_t, info, w1, w3, w2, final_gain=None):
    T, D = x2.shape
    E, _, H = w1.shape
    tm = min(MOE_TM, T)
    nt = T // tm
    n_ffn = -(-(nt * MOE_R + E * (MOE_TME - 1)) // MOE_TME)
    blk_dst, n_blk, tile_expert, n_used, zero_start, zero_count = _moe_tables(info, n_ffn)
    rows = n_ffn * MOE_TME

    xs = pl.pallas_call(
        _dispatch_kernel,
        grid_spec=pltpu.PrefetchScalarGridSpec(
            num_scalar_prefetch=4, grid=(nt,),
            in_specs=[pl.BlockSpec((tm, D), lambda i, *_: (i, 0)),
                      pl.BlockSpec((tm, LANES), lambda i, *_: (i, 0)),
                      pl.BlockSpec((1, 8, tm), lambda i, *_: (i, 0, 0))],
            out_specs=pl.BlockSpec(memory_space=pl.ANY),
            scratch_shapes=[pltpu.VMEM((2, MOE_R, MOE_XW), BF16), pltpu.VMEM((MOE_ALIGN, MOE_XW), BF16),
                            pltpu.SemaphoreType.DMA((2,))]),
        out_shape=jax.ShapeDtypeStruct((rows, MOE_XW), BF16),
        compiler_params=_params("arbitrary"),
        name="moe_dispatch",
    )(blk_dst, n_blk, zero_start, zero_count, h, tok, tok_t)

    last = lambda n, nu: jnp.minimum(n, nu[0] - 1)
    ys = pl.pallas_call(
        _ffn_kernel,
        grid_spec=pltpu.PrefetchScalarGridSpec(
            num_scalar_prefetch=2, grid=(n_ffn,),
            in_specs=[pl.BlockSpec((MOE_TME, MOE_XW), lambda n, te, nu: (last(n, nu), 0)),
                      pl.BlockSpec((1, D, H), lambda n, te, nu: (te[n], 0, 0)),
                      pl.BlockSpec((1, D, H), lambda n, te, nu: (te[n], 0, 0)),
                      pl.BlockSpec((1, H, D), lambda n, te, nu: (te[n], 0, 0))],
            out_specs=pl.BlockSpec((MOE_TME, D), lambda n, te, nu: (n, 0))),
        out_shape=jax.ShapeDtypeStruct((rows, D), BF16),
        compiler_params=_params("arbitrary"),
        name="moe_ffn",
    )(tile_expert, n_used, xs, w1, w3, w2)

    fg = jnp.ones((1, D), F32) if final_gain is None else final_gain.reshape(1, D).astype(F32)
    return pl.pallas_call(
        functools.partial(_combine_kernel, final=final_gain is not None),
        grid_spec=pltpu.PrefetchScalarGridSpec(
            num_scalar_prefetch=2, grid=(nt,),
            in_specs=[pl.BlockSpec((tm, D), lambda i, *_: (i, 0)),
                      pl.BlockSpec((tm, LANES), lambda i, *_: (i, 0)),
                      pl.BlockSpec((1, D), lambda i, *_: (0, 0)),
                      pl.BlockSpec(memory_space=pl.ANY)],
            out_specs=pl.BlockSpec((tm, D), lambda i, *_: (i, 0)),
            scratch_shapes=[pltpu.VMEM((2, MOE_R, D), BF16), pltpu.SemaphoreType.DMA((2,))]),
        out_shape=jax.ShapeDtypeStruct((T, D), F32),
        compiler_params=_params("arbitrary"),
        name="moe_combine",
    )(blk_dst, n_blk, x2, tok, fg, ys)


def _segments(widths):
    starts = np.cumsum([0] + list(widths[:-1]))
    return [(int(s), int(w)) for s, w in zip(starts, widths)]


def kernel(x, mem, norm1, norm2, mem_norm, final_norm, mem_w_kv, hgrn_w_in, hgrn_onorm, hgrn_lb_logits, hgrn_w_out, sb_w_in, sb_w_out, dil_w_in, dil_w_out, moe_w_group, moe_b_group, moe_w_expert, moe_b_expert, moe_w1, moe_w3, moe_w2):
    B, S, D = x.shape
    depth = norm1.shape[0]
    T = B * S
    x2 = x.reshape(T, D)

    lb_w = jax.nn.softmax(hgrn_lb_logits.astype(F32), axis=0)
    lower_bounds = jnp.maximum(jnp.cumsum(lb_w, axis=0) - lb_w[:1], 0.0)

    M = mem.shape[1]
    w_kv = mem_w_kv.transpose(1, 0, 2).reshape(D, depth * 2 * MEM_W).astype(BF16)
    mem_kv = _norm_proj(mem.reshape(B * M, D), mem_norm, w_kv, _segments([MEM_W] * (2 * depth)),
                        [BF16] * (2 * depth))

    for layer in range(depth):
        kind, j = layer % N_MIXERS, layer // N_MIXERS
        mem_k = mem_kv[2 * layer].reshape(B, M, MEM_W)
        mem_v = mem_kv[2 * layer + 1].reshape(B, M, MEM_W)
        seq = lambda a: a.reshape(B, S, a.shape[-1])
        if kind == 0:
            q, f, i, g, mq = _norm_proj(x2, norm1[layer], hgrn_w_in[j].astype(BF16),
                                        _segments([MIX_W] * 4 + [MEM_W]), [BF16, F32, BF16, BF16, BF16])
            mix = _hgrn_mixer(seq(q), seq(f), seq(i), seq(g), lower_bounds[layer], hgrn_onorm[j])
            mix_parts, w_out = [mix.reshape(T, MIX_W)], hgrn_w_out[j]
        elif kind == 1:
            q, k, v, mq = _norm_proj(x2, norm1[layer], sb_w_in[j].astype(BF16),
                                     _segments([MIX_W] * 3 + [MEM_W]), [BF16] * 4)
            mix = _sb_mixer(seq(q), seq(k), seq(v))
            mix_parts, w_out = [mix.reshape(T, MIX_W)], sb_w_out[j]
        else:
            n_grp = len(DIL_GROUPS)
            *qkv, mq = _norm_proj(x2, norm1[layer], dil_w_in[j].astype(BF16),
                                  _segments([DIL_OUT_W] * (3 * n_grp) + [MEM_W]), [BF16] * (3 * n_grp + 1))
            parts = [_dil_group(seq(qkv[grp]), seq(qkv[n_grp + grp]), seq(qkv[2 * n_grp + grp]), grp)
                     for grp in range(n_grp)]
            mix_parts = ([o.reshape(T, DIL_OUT_W) for o, _ in parts]
                         + [l.reshape(T, DIL_OUT_W) for _, l in parts])
            w_out = dil_w_out[j]
        mem_out = _mem_attention(seq(mq), mem_k, mem_v).reshape(T, MEM_W)

        pad = LANES - MOE_GROUPS - MOE_EXPERTS
        w_router = jnp.pad(jnp.concatenate([moe_w_group[layer], moe_w_expert[layer]], axis=1),
                           ((0, 0), (0, pad))).astype(F32)
        b_router = jnp.pad(jnp.concatenate([moe_b_group[layer], moe_b_expert[layer]]),
                           (0, pad)).astype(F32).reshape(1, LANES)
        x2, h, tok, tok_t, info = _out_proj_router(mix_parts, mem_out, x2, w_out.astype(BF16),
                                                   norm2[layer], w_router, b_router)
        x2 = _moe_experts(x2, h, tok, tok_t, info, moe_w1[layer].astype(BF16),
                          moe_w3[layer].astype(BF16), moe_w2[layer].astype(BF16),
                          final_gain=final_norm if layer == depth - 1 else None)

    return x2.reshape(B, S, D)
```

```python
import functools

import numpy as np
import jax
import jax.numpy as jnp
from jax import lax
from jax.experimental import pallas as pl
from jax.experimental.pallas import tpu as pltpu

D_MODEL = 1024
HEAD_DIM = 64
MEM_HEADS = 4
MEM_W = MEM_HEADS * HEAD_DIM
MIX_W = D_MODEL - MEM_W
N_MIXERS = 3

HGRN_HEAD_DIM = 128
HGRN_HEADS = MIX_W // HGRN_HEAD_DIM
LB_EPS = 1e-30

DIL_GROUPS = ((128, 1), (512, 4), (2048, 16))
DIL_HEADS = MIX_W // HEAD_DIM
DIL_HEADS_PER_GROUP = DIL_HEADS // len(DIL_GROUPS)
DIL_OUT_W = DIL_HEADS_PER_GROUP * HEAD_DIM
DIL_BLOCK = 128
MASK_VALUE = -1e30

MOE_GROUPS = 4
MOE_EXPERTS_PER_GROUP = 4
MOE_EXPERTS = MOE_GROUPS * MOE_EXPERTS_PER_GROUP
MOE_HIDDEN = 512
RMS_EPS = 1e-6

LANES = 128
VMEM_LIMIT = 48 * 1024 * 1024

F32_EXP_UNDERFLOW = -104.0

BF16 = jnp.bfloat16
F32 = jnp.float32
NT_DIMS = (((1,), (1,)), ((), ()))


def _params(*semantics):
    return pltpu.CompilerParams(dimension_semantics=semantics, vmem_limit_bytes=VMEM_LIMIT)


def _log_sigmoid(z):
    return jnp.minimum(z, 0.0) - jnp.log(1.0 + jnp.exp(-jnp.abs(z)))


def _silu(z):
    return z / (1.0 + jnp.exp(-z))


def _rms(x, gain):
    return x * lax.rsqrt(jnp.mean(x * x, axis=-1, keepdims=True) + RMS_EPS) * gain


def _norm_proj_kernel(x_ref, g_ref, w_ref, *out_refs, segs):
    h = _rms(x_ref[...], g_ref[...]).astype(BF16)
    for (start, width), o_ref in zip(segs, out_refs):
        o_ref[...] = jnp.dot(h, w_ref[:, start:start + width],
                             preferred_element_type=F32).astype(o_ref.dtype)


def _norm_proj(x2, gain, w, segs, dtypes, tm=512):
    T, D = x2.shape
    N = w.shape[1]
    tm = min(tm, T)
    return pl.pallas_call(
        functools.partial(_norm_proj_kernel, segs=tuple(segs)),
        grid=(T // tm,),
        in_specs=[pl.BlockSpec((tm, D), lambda i: (i, 0)),
                  pl.BlockSpec((1, D), lambda i: (0, 0)),
                  pl.BlockSpec((D, N), lambda i: (0, 0))],
        out_specs=[pl.BlockSpec((tm, wd), lambda i: (i, 0)) for _, wd in segs],
        out_shape=[jax.ShapeDtypeStruct((T, wd), dt) for (_, wd), dt in zip(segs, dtypes)],
        compiler_params=_params("parallel"),
        name="norm_proj",
    )(x2, gain.reshape(1, D), w)


HGRN_C = 128
HGRN_LEVELS = 7


def _hgrn_level_matrix():
    C = HGRN_C
    m = np.zeros((HGRN_LEVELS + 1, C, C), np.float32)
    j = np.arange(C)
    for lvl in range(HGRN_LEVELS):
        bs = C >> lvl
        half = bs // 2
        for t in range(C):
            r = (t // bs) * bs + half - 1
            if t & half:
                m[lvl, t] = (j > r) & (j <= t)
            else:
                m[lvl, t] = (j > t) & (j <= r)
    m[HGRN_LEVELS] = j[None, :] <= j[:, None]
    return m.reshape((HGRN_LEVELS + 1) * C, C)


def _hgrn_kernel(q_ref, f_ref, i_ref, g_ref, lb_ref, gain_ref, m_ref, o_ref, st_ref, *, n_chunks):
    C = HGRN_C

    @pl.when(pl.program_id(2) == 0)
    def _():
        st_ref[...] = jnp.zeros_like(st_ref)

    row = lax.broadcasted_iota(jnp.int32, (C, C), 0)
    col = lax.broadcasted_iota(jnp.int32, (C, C), 1)
    rcol = lax.broadcasted_iota(jnp.int32, (C, 1), 0)
    log_lb = jnp.log(lb_ref[...] + LB_EPS)
    gain = gain_ref[...]

    chunks = []
    for c in range(n_chunks):
        sl = slice(c * C, (c + 1) * C)
        z = f_ref[0, sl, :]
        a1 = _log_sigmoid(z)
        a2 = log_lb + (a1 - z)
        log_f = jnp.maximum(a1, a2) + jnp.log(1.0 + jnp.exp(-jnp.abs(a1 - a2)))
        log_f = jnp.minimum(log_f, 0.0)
        k = 1.0 - jnp.exp(log_f)
        q = _silu(q_ref[0, sl, :].astype(F32)) * (HGRN_HEAD_DIM ** -0.5)
        g_hi = log_f.astype(BF16)
        g_lo = (log_f - g_hi.astype(F32)).astype(BF16)
        e_all = jnp.dot(m_ref[...], jnp.concatenate([g_hi, g_lo], axis=1),
                        preferred_element_type=F32)
        e_all = e_all[:, :HGRN_HEAD_DIM] + e_all[:, HGRN_HEAD_DIM:]
        chunks.append(dict(sl=sl, k=k, q=q, v=i_ref[0, sl, :], e_all=e_all))

    for ch in chunks:
        q, k, e_all = ch["q"], ch["k"], ch["e_all"]
        pairs = [(q.astype(BF16), k.astype(BF16))]
        for lvl in range(HGRN_LEVELS):
            half = (C >> lvl) // 2
            dec = jnp.exp(e_all[lvl * C:(lvl + 1) * C])
            upper = (rcol & half) != 0
            pairs.append((jnp.where(upper, q * dec, 0.0).astype(BF16),
                          jnp.where(upper, 0.0, k * dec).astype(BF16)))
        ch["pairs"] = pairs
    for ch in chunks:
        ch["p"] = [lax.dot_general(qt, kt, NT_DIMS, preferred_element_type=F32) for qt, kt in ch["pairs"]]
    for ch in chunks:
        attn = jnp.where(row == col, ch["p"][0], 0.0)
        for lvl in range(HGRN_LEVELS):
            attn = attn + jnp.where((row ^ col) < (C >> lvl), ch["p"][lvl + 1], 0.0)
        b = ch["e_all"][HGRN_LEVELS * C:]
        b_end = b[C - 1:C, :]
        ch["attn"] = attn.astype(BF16)
        ch["q_dec"] = (ch["q"] * jnp.exp(b)).astype(BF16)
        ch["k_end"] = (ch["k"] * jnp.exp(b_end - b)).astype(BF16)
        ch["v_t"] = ch["v"].astype(F32).T.astype(BF16)
        ch["keep"] = jnp.exp(b_end)
    for ch in chunks:
        ch["o"] = jnp.dot(ch["attn"], ch["v"], preferred_element_type=F32)
        ch["kv"] = jnp.dot(ch["v_t"], ch["k_end"], preferred_element_type=F32)

    st = st_ref[...]
    for ch in chunks:
        ch["o"] = ch["o"] + lax.dot_general(ch["q_dec"], st.astype(BF16), NT_DIMS,
                                            preferred_element_type=F32)
        st = st * ch["keep"] + ch["kv"]
    st_ref[...] = st

    for ch in chunks:
        gate = _silu(g_ref[0, ch["sl"], :].astype(F32))
        o_ref[0, ch["sl"], :] = (_rms(ch["o"], gain) * gate).astype(o_ref.dtype)


def _hgrn_mixer(q, f, i, g, lower_bound, out_gain, ts=1024):
    B, S, W = q.shape
    dh = HGRN_HEAD_DIM
    ts = min(ts, S)
    seq = pl.BlockSpec((1, ts, dh), lambda b, h, s: (b, s, h))
    m = jnp.asarray(_hgrn_level_matrix(), BF16)
    return pl.pallas_call(
        functools.partial(_hgrn_kernel, n_chunks=ts // HGRN_C),
        grid=(B, W // dh, S // ts),
        in_specs=[seq, seq, seq, seq,
                  pl.BlockSpec((1, dh), lambda b, h, s: (0, h)),
                  pl.BlockSpec((1, dh), lambda b, h, s: (0, 0)),
                  pl.BlockSpec(m.shape, lambda b, h, s: (0, 0))],
        out_specs=seq,
        out_shape=jax.ShapeDtypeStruct((B, S, W), BF16),
        scratch_shapes=[pltpu.VMEM((dh, dh), F32)],
        compiler_params=_params("parallel", "parallel", "arbitrary"),
        name="hgrn_mixer",
    )(q, f, i, g, lower_bound.reshape(1, W), out_gain.reshape(1, dh), m)


SB_BLOCK = 256
SB_HALF = LANES


def _sb_kernel(q_ref, k_ref, v_ref, o_ref, *, n_blocks):
    BQ, HB = SB_BLOCK, SB_HALF
    heads = [slice(h * HEAD_DIM, (h + 1) * HEAD_DIM) for h in range(LANES // HEAD_DIM)]
    row = lax.broadcasted_iota(jnp.int32, (BQ, BQ), 0)
    col = lax.broadcasted_iota(jnp.int32, (BQ, BQ), 1)
    tr = lax.broadcasted_iota(jnp.int32, (HB, 2 * HB), 0)
    tc = lax.broadcasted_iota(jnp.int32, (HB, 2 * HB), 1)
    suffix = jnp.where((tc >= HB) | (tr > tc), 1.0, 0.0).astype(BF16)

    def step(qs, k0, accs, cs, diagonal):
        earlier = col < row
        zs = [lax.dot_general(q, k_ref[0, pl.ds(k0, BQ), hs], NT_DIMS, preferred_element_type=F32)
              for q, hs in zip(qs, heads)]
        lss, lhss = [], []
        for z in zs:
            ls = _log_sigmoid(z)
            lk = ls - z
            if diagonal:
                lk = jnp.where(earlier, lk, 0.0)
            lk = lk.astype(BF16)
            lss.append(ls)
            lhss.append(jnp.concatenate([lk[:, HB:], lk[:, :HB]], axis=0))
        ress = [jnp.dot(lhs, suffix, preferred_element_type=F32) for lhs in lhss]
        weights, new_cs = [], []
        for ls, res, c in zip(lss, ress, cs):
            res_r = res[:BQ]
            res_l = res[BQ:]
            stick_r = res_r[:, :HB] + c
            c = c + res_r[:, HB:]
            stick_l = res_l[:, :HB] + c
            new_cs.append(c + res_l[:, HB:])
            a = jnp.exp(ls + jnp.concatenate([stick_l, stick_r], axis=1))
            if diagonal:
                a = jnp.where(earlier, a, 0.0)
            weights.append(a.astype(BF16))
        accs = [acc + jnp.dot(a, v_ref[0, pl.ds(k0, BQ), hs], preferred_element_type=F32)
                for acc, a, hs in zip(accs, weights, heads)]
        return accs, new_cs

    def keep_going(c0, c1):
        return (jnp.max(jnp.maximum(c0, c1)) > F32_EXP_UNDERFLOW).astype(jnp.int32)

    def q_block(qb, _):
        q0 = pl.multiple_of(qb * BQ, BQ)
        qs = [q_ref[0, pl.ds(q0, BQ), hs] * (HEAD_DIM ** -0.5) for hs in heads]
        (acc0, acc1), (c0, c1) = step(qs, q0, [jnp.zeros((BQ, HEAD_DIM), F32)] * 2,
                                      [jnp.zeros((BQ, HB), F32)] * 2, True)

        def cond(state):
            return jnp.logical_and(state[0] <= qb, state[-1] > 0)

        def body(state):
            j, acc0, acc1, c0, c1, _ = state
            k0 = pl.multiple_of((qb - j) * BQ, BQ)
            (acc0, acc1), (c0, c1) = step(qs, k0, [acc0, acc1], [c0, c1], False)
            return j + 1, acc0, acc1, c0, c1, keep_going(c0, c1)

        state = lax.while_loop(cond, body, (jnp.int32(1), acc0, acc1, c0, c1, keep_going(c0, c1)))
        o_ref[0, pl.ds(q0, BQ), :] = jnp.concatenate(state[1:3], axis=1).astype(o_ref.dtype)
        return 0

    lax.fori_loop(0, n_blocks, q_block, 0)


def _sb_mixer(q, k, v):
    B, S, W = q.shape
    spec = pl.BlockSpec((1, S, LANES), lambda b, h: (b, 0, h))
    return pl.pallas_call(
        functools.partial(_sb_kernel, n_blocks=S // SB_BLOCK),
        grid=(B, W // LANES),
        in_specs=[spec, spec, spec],
        out_specs=spec,
        out_shape=jax.ShapeDtypeStruct((B, S, W), BF16),
        compiler_params=_params("parallel", "parallel"),
        name="sb_mixer",
    )(q, k, v)


def _alibi_slope(head):
    return float(np.exp2(-8.0 * (head + 1) / DIL_HEADS))


DIL_STEP_BLOCKS = 4


def _dil_kernel(q_ref, kp_ref, kc_ref, vp_ref, vc_ref, o_ref, l_ref, *, group, dil, n_back):
    BQ = DIL_BLOCK
    n = pl.program_id(2)
    row = lax.broadcasted_iota(jnp.int32, (BQ, BQ), 0)
    col = lax.broadcasted_iota(jnp.int32, (BQ, BQ), 1)
    off_c = row - col
    off_p = off_c + BQ
    valid_c = (off_c >= 0) & (off_c <= n_back)
    W = q_ref.shape[-1]
    head_of = lax.broadcasted_iota(jnp.int32, (BQ, W), 1) // HEAD_DIM
    ones = jnp.ones((BQ, LANES), BF16)
    items = []
    for sb in range(DIL_STEP_BLOCKS):
        rows = slice(sb * BQ, (sb + 1) * BQ)
        if sb == 0:
            valid_p = (off_p <= n_back) & (n > 0)
            k_prev, v_prev = kp_ref, vp_ref
            prev_rows = slice(0, BQ)
        else:
            valid_p = off_p <= n_back
            k_prev, v_prev = kc_ref, vc_ref
            prev_rows = slice((sb - 1) * BQ, sb * BQ)
        q_all = q_ref[0, rows, :] * (HEAD_DIM ** -0.5)
        blk = dict(rows=rows, valid_p=valid_p, q=q_all, k_c=kc_ref[0, rows, :], k_p=k_prev[0, prev_rows, :],
                   v_c=vc_ref[0, rows, :], v_p=v_prev[0, prev_rows, :])
        items += [(blk, hh) for hh in range(DIL_HEADS_PER_GROUP)]

    scores = []
    for blk, hh in items:
        q = jnp.where(head_of == hh, blk["q"], jnp.zeros_like(blk["q"]))
        scores.append((lax.dot_general(q, blk["k_c"], NT_DIMS, preferred_element_type=F32),
                       lax.dot_general(q, blk["k_p"], NT_DIMS, preferred_element_type=F32)))
    masked = []
    for (blk, hh), (s_c, s_p) in zip(items, scores):
        slope = _alibi_slope(group * DIL_HEADS_PER_GROUP + hh) * dil
        s_c = jnp.where(valid_c, s_c - slope * off_c.astype(F32), MASK_VALUE)
        s_p = jnp.where(blk["valid_p"], s_p - slope * off_p.astype(F32), MASK_VALUE)
        masked.append((s_c, s_p, jnp.max(jnp.maximum(s_c, s_p), axis=1, keepdims=True)))
    weights = []
    for (blk, hh), (s_c, s_p, m) in zip(items, masked):
        weights.append((jnp.where(valid_c, jnp.exp(s_c - m), 0.0).astype(BF16),
                        jnp.where(blk["valid_p"], jnp.exp(s_p - m), 0.0).astype(BF16)))
    sums = []
    for (blk, hh), (e_c, e_p) in zip(items, weights):
        o = (jnp.dot(e_c, blk["v_c"], preferred_element_type=F32)
             + jnp.dot(e_p, blk["v_p"], preferred_element_type=F32))
        den = (jnp.dot(e_c, ones, preferred_element_type=F32)
               + jnp.dot(e_p, ones, preferred_element_type=F32))
        sums.append((o, jnp.concatenate([den] * (W // LANES), axis=1)))
    for sb in range(DIL_STEP_BLOCKS):
        out = jnp.zeros((BQ, W), F32)
        lse = jnp.zeros((BQ, W), F32)
        for idx in range(sb * DIL_HEADS_PER_GROUP, (sb + 1) * DIL_HEADS_PER_GROUP):
            (blk, hh), (o, den), m = items[idx], sums[idx], masked[idx][2]
            out = jnp.where(head_of == hh, o / den, out)
            lse = jnp.where(head_of == hh, m + jnp.log(den), lse)
        o_ref[0, blk["rows"], :] = out.astype(o_ref.dtype)
        l_ref[0, blk["rows"], :] = lse


def _dil_group(q, k, v, group):
    window, dil = DIL_GROUPS[group]
    B, S, W = q.shape
    L = S // dil
    nb = L // DIL_BLOCK
    view = lambda a: a.reshape(B, L, dil * W)
    sbk = DIL_STEP_BLOCKS
    cur = pl.BlockSpec((1, sbk * DIL_BLOCK, DIL_OUT_W), lambda b, r, n: (b, n, r))
    prev = pl.BlockSpec((1, DIL_BLOCK, DIL_OUT_W), lambda b, r, n: (b, jnp.maximum(sbk * n - 1, 0), r))
    out = cur
    o, lse = pl.pallas_call(
        functools.partial(_dil_kernel, group=group, dil=dil, n_back=window // dil),
        grid=(B, dil, nb // sbk),
        in_specs=[cur, prev, cur, prev, cur],
        out_specs=[out, out],
        out_shape=[jax.ShapeDtypeStruct((B, L, dil * DIL_OUT_W), BF16),
                   jax.ShapeDtypeStruct((B, L, dil * DIL_OUT_W), F32)],
        compiler_params=_params("parallel", "parallel", "parallel"),
        name=f"dil_group{group}",
    )(view(q), view(k), view(k), view(v), view(v))
    return o.reshape(B, S, DIL_OUT_W), lse.reshape(B, S, DIL_OUT_W)


def _mem_attn_kernel(q_ref, k_ref, v_ref, o_ref):
    heads = [slice(hh * HEAD_DIM, (hh + 1) * HEAD_DIM) for hh in range(MEM_HEADS)]
    ss = [lax.dot_general(q_ref[0, :, hs] * (HEAD_DIM ** -0.5), k_ref[0, :, hs], NT_DIMS,
                          preferred_element_type=F32) for hs in heads]
    es = [jnp.exp(s - jnp.max(s, axis=1, keepdims=True)) for s in ss]
    os_ = [jnp.dot(e.astype(BF16), v_ref[0, :, hs], preferred_element_type=F32) for e, hs in zip(es, heads)]
    outs = [o / jnp.sum(e, axis=1, keepdims=True) for o, e in zip(os_, es)]
    o_ref[0] = jnp.concatenate(outs, axis=1).astype(o_ref.dtype)


def _mem_attention(q, mem_k, mem_v, tm=1024):
    B, S, W = q.shape
    M = mem_k.shape[1]
    tm = min(tm, S)
    tok = pl.BlockSpec((1, tm, W), lambda b, s: (b, s, 0))
    kv = pl.BlockSpec((1, M, W), lambda b, s: (b, 0, 0))
    return pl.pallas_call(
        _mem_attn_kernel,
        grid=(B, S // tm),
        in_specs=[tok, kv, kv],
        out_specs=tok,
        out_shape=jax.ShapeDtypeStruct((B, S, W), BF16),
        compiler_params=_params("parallel", "parallel"),
        name="mem_attention",
    )(q, mem_k, mem_v)


MOE_TM = 512
MOE_ALIGN = 16
MOE_R = 2 * MOE_TM + MOE_EXPERTS * MOE_ALIGN
MOE_NB = MOE_R // MOE_ALIGN
MOE_XW = D_MODEL + LANES
MOE_TME = 512
EXPERT_LANE0 = MOE_GROUPS


def _route(h, wr_ref, b_ref, tri_ref):
    tm = h.shape[0]
    h_hi = h.astype(BF16)
    h_lo = (h - h_hi.astype(F32)).astype(BF16)
    lg = jnp.dot(h_hi, wr_ref[...], preferred_element_type=F32)
    logits = (lg[:, :LANES] + lg[:, LANES:]
              + jnp.dot(h_lo, wr_ref[:, :LANES], preferred_element_type=F32) + b_ref[...])
    lane = lax.broadcasted_iota(jnp.int32, logits.shape, 1)
    neg = jnp.float32(-1e30)

    def first_argmax(vals):
        mx = jnp.max(vals, axis=1, keepdims=True)
        return mx, jnp.min(jnp.where(vals == mx, lane, LANES), axis=1, keepdims=True)

    glog = jnp.where(lane < MOE_GROUPS, logits, neg)
    gmax, gidx = first_argmax(glog)
    group_gate = 1.0 / jnp.sum(jnp.where(lane < MOE_GROUPS, jnp.exp(glog - gmax), 0.0),
                               axis=1, keepdims=True)
    lo = MOE_GROUPS + gidx * MOE_EXPERTS_PER_GROUP
    elog = jnp.where((lane >= lo) & (lane < lo + MOE_EXPERTS_PER_GROUP), logits, neg)
    m1, i1 = first_argmax(elog)
    m2, i2 = first_argmax(jnp.where(lane == i1, neg, elog))
    g1 = group_gate / (1.0 + jnp.exp(m2 - m1))
    g2 = group_gate - g1

    chosen = jnp.where((lane == i1) | (lane == i2), 1.0, 0.0)
    earlier = jnp.dot(tri_ref[...], chosen.astype(BF16), preferred_element_type=F32)
    counts = jnp.sum(chosen, axis=0, keepdims=True)
    padded = jnp.floor((counts + (MOE_ALIGN - 1)) * (1.0 / MOE_ALIGN)) * MOE_ALIGN
    ur = lax.broadcasted_iota(jnp.int32, (LANES, LANES), 0)
    uc = lax.broadcasted_iota(jnp.int32, (LANES, LANES), 1)
    before = jnp.where(ur < uc, 1.0, 0.0).astype(BF16)
    sub8 = lax.broadcasted_iota(jnp.int32, (8, LANES), 0)
    padded8 = jnp.broadcast_to(padded, (8, LANES))
    start8 = jnp.dot(padded8.astype(BF16), before, preferred_element_type=F32)
    pos = earlier + start8[0:1]
    r1 = jnp.sum(jnp.where(lane == i1, pos, 0.0), axis=1, keepdims=True)
    r2 = jnp.sum(jnp.where(lane == i2, pos, 0.0), axis=1, keepdims=True)
    tok = jnp.where(lane == 0, r1, jnp.where(lane == 1, r2, jnp.where(lane == 2, g1,
                                                                      jnp.where(lane == 3, g2, 0.0))))
    tile_info = jnp.where(sub8 == 0, padded8, jnp.where(sub8 == 1, start8, 0.0))
    return tok, tok.T[:8], tile_info


def _mix_from_refs(mix_refs):
    if len(mix_refs) == 1:
        return mix_refs[0][...]
    o0, o1, o2, l0, l1, l2 = [r[...] for r in mix_refs]
    m = jnp.maximum(jnp.maximum(l0, l1), l2)
    w0, w1, w2 = jnp.exp(l0 - m), jnp.exp(l1 - m), jnp.exp(l2 - m)
    mix = (w0 * o0.astype(F32) + w1 * o1.astype(F32) + w2 * o2.astype(F32)) / (w0 + w1 + w2)
    return mix.astype(BF16)


def _out_proj_router_kernel(*refs, n_mix):
    mix_refs = refs[:n_mix]
    mem_ref, x_ref, w_ref, g_ref, wr_ref, b_ref, tri_ref = refs[n_mix:n_mix + 7]
    xo_ref, h_ref, tok_ref, tokt_ref, info_ref = refs[n_mix + 7:]
    mix = _mix_from_refs(mix_refs)
    nm = mix.shape[-1]
    x = (x_ref[...] + jnp.dot(mix, w_ref[:nm, :], preferred_element_type=F32)
         + jnp.dot(mem_ref[...], w_ref[nm:, :], preferred_element_type=F32))
    xo_ref[...] = x
    h = _rms(x, g_ref[...])
    h_ref[...] = h.astype(h_ref.dtype)
    tok, tok_t, info = _route(h, wr_ref, b_ref, tri_ref)
    tok_ref[...] = tok
    tokt_ref[0] = tok_t
    info_ref[0] = info


def _out_proj_router(mix_parts, mem_out, x2, w_out, gain, w_router, b_router):
    T, D = x2.shape
    tm = min(MOE_TM, T)
    nt = T // tm
    tok = lambda wd: pl.BlockSpec((tm, wd), lambda i: (i, 0))
    const = lambda a: pl.BlockSpec(a.shape, lambda i: (0,) * a.ndim)
    w_hi = w_router.astype(BF16)
    w_lo = (w_router - w_hi.astype(F32)).astype(BF16)
    wr = jnp.concatenate([w_hi, w_lo], axis=1)
    tri = jnp.asarray(np.tril(np.ones((tm, tm), np.float32), -1), BF16)
    ins = list(mix_parts) + [mem_out, x2]
    consts = [w_out, gain.reshape(1, D), wr, b_router, tri]
    return pl.pallas_call(
        functools.partial(_out_proj_router_kernel, n_mix=len(mix_parts)),
        grid=(nt,),
        in_specs=[tok(a.shape[1]) for a in ins] + [const(a) for a in consts],
        out_specs=[tok(D), tok(D), tok(LANES),
                   pl.BlockSpec((1, 8, tm), lambda i: (i, 0, 0)),
                   pl.BlockSpec((1, 8, LANES), lambda i: (i, 0, 0))],
        out_shape=[jax.ShapeDtypeStruct((T, D), F32), jax.ShapeDtypeStruct((T, D), BF16),
                   jax.ShapeDtypeStruct((T, LANES), F32), jax.ShapeDtypeStruct((nt, 8, tm), F32),
                   jax.ShapeDtypeStruct((nt, 8, LANES), F32)],
        compiler_params=_params("parallel"),
        name="out_proj_router",
    )(*ins, *consts)


def _moe_tables(info, n_ffn_tiles):
    E = MOE_EXPERTS
    lanes = slice(EXPERT_LANE0, EXPERT_LANE0 + E)
    rows_te = info[:, 0, lanes].astype(jnp.int32)
    start_te = info[:, 1, lanes].astype(jnp.int32)
    end_te = start_te + rows_te
    n_blk = end_te[:, -1] // MOE_ALIGN
    total = rows_te.sum(axis=0)
    cap = -(-total // MOE_TME) * MOE_TME
    cap_end = jnp.cumsum(cap)
    base = cap_end - cap
    run = jnp.cumsum(rows_te, axis=0) - rows_te
    rows = jnp.arange(MOE_NB, dtype=jnp.int32) * MOE_ALIGN
    e_of = jnp.sum(rows[None, :, None] >= end_te[:, None, :], axis=-1)
    in_e = e_of[:, :, None] == jnp.arange(E, dtype=jnp.int32)
    shift = base[None, :] + run - start_te
    dst_row = rows[None, :] + jnp.sum(jnp.where(in_e, shift[:, None, :], 0), axis=-1)
    blk_dst = dst_row // MOE_ALIGN
    n_used = (cap_end[-1] // MOE_TME).reshape(1)
    tile_row = jnp.arange(n_ffn_tiles, dtype=jnp.int32) * MOE_TME
    tile_expert = jnp.minimum(jnp.sum(tile_row[:, None] >= cap_end[None, :], axis=-1), E - 1)
    spare = jnp.full((1,), n_ffn_tiles * MOE_TME, jnp.int32)
    zero_start = jnp.concatenate([base + total, cap_end[-1:], spare]) // MOE_ALIGN
    zero_count = jnp.concatenate([cap - total, n_ffn_tiles * MOE_TME - cap_end[-1:],
                                  jnp.full((1,), 2 * MOE_R, jnp.int32)]) // MOE_ALIGN
    i32 = lambda a: a.astype(jnp.int32)
    return (i32(blk_dst.reshape(-1)), i32(n_blk), i32(tile_expert), i32(n_used),
            i32(zero_start), i32(zero_count))


def _tile_copies(copy, whole, t, slot):
    def start_all():
        def body(k, _):
            copy(t, slot, k).start()
            return 0
        lax.fori_loop(0, MOE_NB, body, 0)
    return start_all, (lambda: whole(slot).wait())


def _dispatch_kernel(dst_ref, nblk_ref, zs_ref, zc_ref, h_ref, tok_ref, tokt_ref, xs_ref,
                     sorted_ref, zero_ref, sems):
    i = pl.program_id(0)
    last = pl.num_programs(0) - 1
    slot = i % 2
    spare_blk = (xs_ref.shape[0] - 2 * MOE_R) // MOE_ALIGN

    def copy(t, s, k):
        blk = jnp.where(k < nblk_ref[t], dst_ref[t * MOE_NB + k], spare_blk + s * MOE_NB + k)
        return pltpu.make_async_copy(
            sorted_ref.at[s, pl.ds(pl.multiple_of(k * MOE_ALIGN, MOE_ALIGN), MOE_ALIGN), :],
            xs_ref.at[pl.ds(pl.multiple_of(blk * MOE_ALIGN, MOE_ALIGN), MOE_ALIGN), :], sems.at[s])

    def whole(s):
        return pltpu.make_async_copy(sorted_ref.at[s], xs_ref.at[pl.ds(0, MOE_R), :], sems.at[s])

    @pl.when(i >= 2)
    def _():
        _tile_copies(copy, whole, i - 2, slot)[1]()

    tm, D = h_ref.shape
    R = sorted_ref.shape[1]
    sub = lax.broadcasted_iota(jnp.int32, (R, tm), 0)
    p1 = sub == tokt_ref[0, 0:1, :].astype(jnp.int32)
    p2 = sub == tokt_ref[0, 1:2, :].astype(jnp.int32)
    both = jnp.where(p1 | p2, 1.0, 0.0).astype(BF16)
    sorted_ref[slot, :, :D] = jnp.dot(both, h_ref[...], preferred_element_type=F32).astype(BF16)
    lane = lax.broadcasted_iota(jnp.int32, (tm, LANES), 1)

    def parts(g):
        hi = g.astype(BF16).astype(F32)
        return jnp.where(lane == 0, hi, jnp.where(lane == 1, g - hi, 0.0)).astype(BF16)

    tok = tok_ref[...]
    gates = (jnp.dot(jnp.where(p1, 1.0, 0.0).astype(BF16), parts(tok[:, 2:3]), preferred_element_type=F32)
             + jnp.dot(jnp.where(p2, 1.0, 0.0).astype(BF16), parts(tok[:, 3:4]), preferred_element_type=F32))
    sorted_ref[slot, :, D:] = gates.astype(BF16)
    _tile_copies(copy, whole, i, slot)[0]()

    @pl.when(i == last)
    def _():
        @pl.when(i >= 1)
        def _():
            _tile_copies(copy, whole, i - 1, 1 - slot)[1]()

        _tile_copies(copy, whole, i, slot)[1]()
        zero_ref[...] = jnp.zeros_like(zero_ref)

        def zero_range(r, _):
            def zero_copy(k):
                dst = pl.multiple_of((zs_ref[r] + k) * MOE_ALIGN, MOE_ALIGN)
                return pltpu.make_async_copy(zero_ref, xs_ref.at[pl.ds(dst, MOE_ALIGN), :], sems.at[0])

            def start(k, _):
                zero_copy(k).start()
                return 0

            def wait(k, _):
                zero_copy(k).wait()
                return 0

            def batch(c, _):
                lo = c * MOE_NB
                hi = jnp.minimum(lo + MOE_NB, zc_ref[r])
                lax.fori_loop(lo, hi, start, 0)
                lax.fori_loop(lo, hi, wait, 0)
                return 0

            lax.fori_loop(0, (zc_ref[r] + MOE_NB - 1) // MOE_NB, batch, 0)
            return 0

        lax.fori_loop(0, zs_ref.shape[0], zero_range, 0)


def _ffn_kernel(te_ref, nu_ref, x_ref, w1_ref, w3_ref, w2_ref, o_ref):
    D = w1_ref.shape[1]

    @pl.when(pl.program_id(0) < nu_ref[0])
    def _():
        x = x_ref[:, :D]
        gate = x_ref[:, D:D + 1].astype(F32) + x_ref[:, D + 1:D + 2].astype(F32)
        a = jnp.dot(x, w1_ref[0], preferred_element_type=F32)
        b = jnp.dot(x, w3_ref[0], preferred_element_type=F32)
        hidden = (_silu(a) * b).astype(BF16)
        o_ref[...] = (jnp.dot(hidden, w2_ref[0], preferred_element_type=F32) * gate).astype(o_ref.dtype)

    @pl.when(pl.program_id(0) >= nu_ref[0])
    def _():
        o_ref[...] = jnp.zeros_like(o_ref)


def _combine_kernel(dst_ref, nblk_ref, x_ref, tok_ref, fg_ref, ys_ref, o_ref, sorted_ref, sems, *, final):
    i = pl.program_id(0)
    slot = i % 2
    tm = x_ref.shape[0]
    R = sorted_ref.shape[1]

    def copy(t, s, k):
        blk = jnp.where(k < nblk_ref[t], dst_ref[t * MOE_NB + k], 0)
        return pltpu.make_async_copy(
            ys_ref.at[pl.ds(pl.multiple_of(blk * MOE_ALIGN, MOE_ALIGN), MOE_ALIGN), :],
            sorted_ref.at[s, pl.ds(pl.multiple_of(k * MOE_ALIGN, MOE_ALIGN), MOE_ALIGN), :], sems.at[s])

    def whole(s):
        return pltpu.make_async_copy(ys_ref.at[pl.ds(0, MOE_R), :], sorted_ref.at[s], sems.at[s])

    @pl.when(i == 0)
    def _():
        _tile_copies(copy, whole, i, slot)[0]()

    @pl.when(i + 1 < pl.num_programs(0))
    def _():
        _tile_copies(copy, whole, i + 1, 1 - slot)[0]()

    _tile_copies(copy, whole, i, slot)[1]()
    lane = lax.broadcasted_iota(jnp.int32, (tm, R), 1)
    tok = tok_ref[...]
    pick = jnp.where((lane == tok[:, 0:1].astype(jnp.int32)) | (lane == tok[:, 1:2].astype(jnp.int32)),
                     1.0, 0.0).astype(BF16)
    y = x_ref[...] + jnp.dot(pick, sorted_ref[slot], preferred_element_type=F32)
    o_ref[...] = _rms(y, fg_ref[...]) if final else y


def _moe_experts(x2, h, tok, tok_t, info, w1, w3, w2, final_gain=None):
    T, D = x2.shape
    E, _, H = w1.shape
    tm = min(MOE_TM, T)
    nt = T // tm
    n_ffn = -(-(nt * MOE_R + E * (MOE_TME - 1)) // MOE_TME)
    blk_dst, n_blk, tile_expert, n_used, zero_start, zero_count = _moe_tables(info, n_ffn)
    rows = n_ffn * MOE_TME

    xs = pl.pallas_call(
        _dispatch_kernel,
        grid_spec=pltpu.PrefetchScalarGridSpec(
            num_scalar_prefetch=4, grid=(nt,),
            in_specs=[pl.BlockSpec((tm, D), lambda i, *_: (i, 0)),
                      pl.BlockSpec((tm, LANES), lambda i, *_: (i, 0)),
                      pl.BlockSpec((1, 8, tm), lambda i, *_: (i, 0, 0))],
            out_specs=pl.BlockSpec(memory_space=pl.ANY),
            scratch_shapes=[pltpu.VMEM((2, MOE_R, MOE_XW), BF16), pltpu.VMEM((MOE_ALIGN, MOE_XW), BF16),
                            pltpu.SemaphoreType.DMA((2,))]),
        out_shape=jax.ShapeDtypeStruct((rows + 2 * MOE_R, MOE_XW), BF16),
        compiler_params=_params("arbitrary"),
        name="moe_dispatch",
    )(blk_dst, n_blk, zero_start, zero_count, h, tok, tok_t)

    last = lambda n, nu: jnp.minimum(n, nu[0] - 1)
    ys = pl.pallas_call(
        _ffn_kernel,
        grid_spec=pltpu.PrefetchScalarGridSpec(
            num_scalar_prefetch=2, grid=(n_ffn,),
            in_specs=[pl.BlockSpec((MOE_TME, MOE_XW), lambda n, te, nu: (last(n, nu), 0)),
                      pl.BlockSpec((1, D, H), lambda n, te, nu: (te[n], 0, 0)),
                      pl.BlockSpec((1, D, H), lambda n, te, nu: (te[n], 0, 0)),
                      pl.BlockSpec((1, H, D), lambda n, te, nu: (te[n], 0, 0))],
            out_specs=pl.BlockSpec((MOE_TME, D), lambda n, te, nu: (n, 0))),
        out_shape=jax.ShapeDtypeStruct((rows, D), BF16),
        compiler_params=_params("arbitrary"),
        name="moe_ffn",
    )(tile_expert, n_used, xs, w1, w3, w2)

    fg = jnp.ones((1, D), F32) if final_gain is None else final_gain.reshape(1, D).astype(F32)
    return pl.pallas_call(
        functools.partial(_combine_kernel, final=final_gain is not None),
        grid_spec=pltpu.PrefetchScalarGridSpec(
            num_scalar_prefetch=2, grid=(nt,),
            in_specs=[pl.BlockSpec((tm, D), lambda i, *_: (i, 0)),
                      pl.BlockSpec((tm, LANES), lambda i, *_: (i, 0)),
                      pl.BlockSpec((1, D), lambda i, *_: (0, 0)),
                      pl.BlockSpec(memory_space=pl.ANY)],
            out_specs=pl.BlockSpec((tm, D), lambda i, *_: (i, 0)),
            scratch_shapes=[pltpu.VMEM((2, MOE_R, D), BF16), pltpu.SemaphoreType.DMA((2,))]),
        out_shape=jax.ShapeDtypeStruct((T, D), F32),
        compiler_params=_params("arbitrary"),
        name="moe_combine",
    )(blk_dst, n_blk, x2, tok, fg, ys)


def _segments(widths):
    starts = np.cumsum([0] + list(widths[:-1]))
    return [(int(s), int(w)) for s, w in zip(starts, widths)]


def kernel(x, mem, norm1, norm2, mem_norm, final_norm, mem_w_kv, hgrn_w_in, hgrn_onorm, hgrn_lb_logits, hgrn_w_out, sb_w_in, sb_w_out, dil_w_in, dil_w_out, moe_w_group, moe_b_group, moe_w_expert, moe_b_expert, moe_w1, moe_w3, moe_w2):
    B, S, D = x.shape
    depth = norm1.shape[0]
    T = B * S
    x2 = x.reshape(T, D)

    lb_w = jax.nn.softmax(hgrn_lb_logits.astype(F32), axis=0)
    lower_bounds = jnp.maximum(jnp.cumsum(lb_w, axis=0) - lb_w[:1], 0.0)

    M = mem.shape[1]
    w_kv = mem_w_kv.transpose(1, 0, 2).reshape(D, depth * 2 * MEM_W).astype(BF16)
    mem_kv = _norm_proj(mem.reshape(B * M, D), mem_norm, w_kv, _segments([MEM_W] * (2 * depth)),
                        [BF16] * (2 * depth))

    for layer in range(depth):
        kind, j = layer % N_MIXERS, layer // N_MIXERS
        mem_k = mem_kv[2 * layer].reshape(B, M, MEM_W)
        mem_v = mem_kv[2 * layer + 1].reshape(B, M, MEM_W)
        seq = lambda a: a.reshape(B, S, a.shape[-1])
        if kind == 0:
            q, f, i, g, mq = _norm_proj(x2, norm1[layer], hgrn_w_in[j].astype(BF16),
                                        _segments([MIX_W] * 4 + [MEM_W]), [BF16, F32, BF16, BF16, BF16])
            mix = _hgrn_mixer(seq(q), seq(f), seq(i), seq(g), lower_bounds[layer], hgrn_onorm[j])
            mix_parts, w_out = [mix.reshape(T, MIX_W)], hgrn_w_out[j]
        elif kind == 1:
            q, k, v, mq = _norm_proj(x2, norm1[layer], sb_w_in[j].astype(BF16),
                                     _segments([MIX_W] * 3 + [MEM_W]), [BF16] * 4)
            mix = _sb_mixer(seq(q), seq(k), seq(v))
            mix_parts, w_out = [mix.reshape(T, MIX_W)], sb_w_out[j]
        else:
            n_grp = len(DIL_GROUPS)
            *qkv, mq = _norm_proj(x2, norm1[layer], dil_w_in[j].astype(BF16),
                                  _segments([DIL_OUT_W] * (3 * n_grp) + [MEM_W]), [BF16] * (3 * n_grp + 1))
            parts = [_dil_group(seq(qkv[grp]), seq(qkv[n_grp + grp]), seq(qkv[2 * n_grp + grp]), grp)
                     for grp in range(n_grp)]
            mix_parts = ([o.reshape(T, DIL_OUT_W) for o, _ in parts]
                         + [l.reshape(T, DIL_OUT_W) for _, l in parts])
            w_out = dil_w_out[j]
        mem_out = _mem_attention(seq(mq), mem_k, mem_v).reshape(T, MEM_W)

        pad = LANES - MOE_GROUPS - MOE_EXPERTS
        w_router = jnp.pad(jnp.concatenate([moe_w_group[layer], moe_w_expert[layer]], axis=1),
                           ((0, 0), (0, pad))).astype(F32)
        b_router = jnp.pad(jnp.concatenate([moe_b_group[layer], moe_b_expert[layer]]),
                           (0, pad)).astype(F32).reshape(1, LANES)
        x2, h, tok, tok_t, info = _out_proj_router(mix_parts, mem_out, x2, w_out.astype(BF16),
                                                   norm2[layer], w_router, b_router)
        x2 = _moe_experts(x2, h, tok, tok_t, info, moe_w1[layer].astype(BF16),
                          moe_w3[layer].astype(BF16), moe_w2[layer].astype(BF16),
                          final_gain=final_norm if layer == depth - 1 else None)

    return x2.reshape(B, S, D)
```

```python
import functools

import numpy as np
import jax
import jax.numpy as jnp
from jax import lax
from jax.experimental import pallas as pl
from jax.experimental.pallas import tpu as pltpu

D_MODEL = 1024
HEAD_DIM = 64
MEM_HEADS = 4
MEM_W = MEM_HEADS * HEAD_DIM
MIX_W = D_MODEL - MEM_W
N_MIXERS = 3

HGRN_HEAD_DIM = 128
HGRN_HEADS = MIX_W // HGRN_HEAD_DIM
LB_EPS = 1e-30

DIL_GROUPS = ((128, 1), (512, 4), (2048, 16))
DIL_HEADS = MIX_W // HEAD_DIM
DIL_HEADS_PER_GROUP = DIL_HEADS // len(DIL_GROUPS)
DIL_OUT_W = DIL_HEADS_PER_GROUP * HEAD_DIM
DIL_BLOCK = 128
MASK_VALUE = -1e30

MOE_GROUPS = 4
MOE_EXPERTS_PER_GROUP = 4
MOE_EXPERTS = MOE_GROUPS * MOE_EXPERTS_PER_GROUP
MOE_HIDDEN = 512
RMS_EPS = 1e-6

LANES = 128
VMEM_LIMIT = 48 * 1024 * 1024

F32_EXP_UNDERFLOW = -104.0

BF16 = jnp.bfloat16
F32 = jnp.float32
NT_DIMS = (((1,), (1,)), ((), ()))


def _params(*semantics):
    return pltpu.CompilerParams(dimension_semantics=semantics, vmem_limit_bytes=VMEM_LIMIT)


def _log_sigmoid(z):
    return jnp.minimum(z, 0.0) - jnp.log(1.0 + jnp.exp(-jnp.abs(z)))


def _silu(z):
    return z / (1.0 + jnp.exp(-z))


def _rms(x, gain):
    return x * lax.rsqrt(jnp.mean(x * x, axis=-1, keepdims=True) + RMS_EPS) * gain


def _norm_proj_kernel(x_ref, g_ref, w_ref, *out_refs, segs):
    h = _rms(x_ref[...], g_ref[...]).astype(BF16)
    for (start, width), o_ref in zip(segs, out_refs):
        o_ref[...] = jnp.dot(h, w_ref[:, start:start + width],
                             preferred_element_type=F32).astype(o_ref.dtype)


def _norm_proj(x2, gain, w, segs, dtypes, tm=512):
    T, D = x2.shape
    N = w.shape[1]
    tm = min(tm, T)
    return pl.pallas_call(
        functools.partial(_norm_proj_kernel, segs=tuple(segs)),
        grid=(T // tm,),
        in_specs=[pl.BlockSpec((tm, D), lambda i: (i, 0)),
                  pl.BlockSpec((1, D), lambda i: (0, 0)),
                  pl.BlockSpec((D, N), lambda i: (0, 0))],
        out_specs=[pl.BlockSpec((tm, wd), lambda i: (i, 0)) for _, wd in segs],
        out_shape=[jax.ShapeDtypeStruct((T, wd), dt) for (_, wd), dt in zip(segs, dtypes)],
        compiler_params=_params("parallel"),
        name="norm_proj",
    )(x2, gain.reshape(1, D), w)


HGRN_C = 128
HGRN_LEVELS = 7


def _hgrn_level_matrix():
    C = HGRN_C
    m = np.zeros((HGRN_LEVELS + 1, C, C), np.float32)
    j = np.arange(C)
    for lvl in range(HGRN_LEVELS):
        bs = C >> lvl
        half = bs // 2
        for t in range(C):
            r = (t // bs) * bs + half - 1
            if t & half:
                m[lvl, t] = (j > r) & (j <= t)
            else:
                m[lvl, t] = (j > t) & (j <= r)
    m[HGRN_LEVELS] = j[None, :] <= j[:, None]
    return m.reshape((HGRN_LEVELS + 1) * C, C)


def _hgrn_kernel(q_ref, f_ref, i_ref, g_ref, lb_ref, gain_ref, m_ref, o_ref, st_ref, *, n_chunks):
    C = HGRN_C

    @pl.when(pl.program_id(2) == 0)
    def _():
        st_ref[...] = jnp.zeros_like(st_ref)

    row = lax.broadcasted_iota(jnp.int32, (C, C), 0)
    col = lax.broadcasted_iota(jnp.int32, (C, C), 1)
    rcol = lax.broadcasted_iota(jnp.int32, (C, 1), 0)
    log_lb = jnp.log(lb_ref[...] + LB_EPS)
    gain = gain_ref[...]

    chunks = []
    for c in range(n_chunks):
        sl = slice(c * C, (c + 1) * C)
        z = f_ref[0, sl, :]
        a1 = _log_sigmoid(z)
        a2 = log_lb + (a1 - z)
        log_f = jnp.maximum(a1, a2) + jnp.log(1.0 + jnp.exp(-jnp.abs(a1 - a2)))
        log_f = jnp.minimum(log_f, 0.0)
        k = 1.0 - jnp.exp(log_f)
        q = _silu(q_ref[0, sl, :].astype(F32)) * (HGRN_HEAD_DIM ** -0.5)
        g_hi = log_f.astype(BF16)
        g_lo = (log_f - g_hi.astype(F32)).astype(BF16)
        e_all = jnp.dot(m_ref[...], jnp.concatenate([g_hi, g_lo], axis=1),
                        preferred_element_type=F32)
        e_all = e_all[:, :HGRN_HEAD_DIM] + e_all[:, HGRN_HEAD_DIM:]
        chunks.append(dict(sl=sl, k=k, q=q, v=i_ref[0, sl, :], e_all=e_all))

    for ch in chunks:
        q, k, e_all = ch["q"], ch["k"], ch["e_all"]
        pairs = [(q.astype(BF16), k.astype(BF16))]
        for lvl in range(HGRN_LEVELS):
            half = (C >> lvl) // 2
            dec = jnp.exp(e_all[lvl * C:(lvl + 1) * C])
            upper = (rcol & half) != 0
            pairs.append((jnp.where(upper, q * dec, 0.0).astype(BF16),
                          jnp.where(upper, 0.0, k * dec).astype(BF16)))
        ch["pairs"] = pairs
    for ch in chunks:
        ch["p"] = [lax.dot_general(qt, kt, NT_DIMS, preferred_element_type=F32) for qt, kt in ch["pairs"]]
    for ch in chunks:
        attn = jnp.where(row == col, ch["p"][0], 0.0)
        for lvl in range(HGRN_LEVELS):
            attn = attn + jnp.where((row ^ col) < (C >> lvl), ch["p"][lvl + 1], 0.0)
        b = ch["e_all"][HGRN_LEVELS * C:]
        b_end = b[C - 1:C, :]
        ch["attn"] = attn.astype(BF16)
        ch["q_dec"] = (ch["q"] * jnp.exp(b)).astype(BF16)
        ch["k_end"] = (ch["k"] * jnp.exp(b_end - b)).astype(BF16)
        ch["v_t"] = ch["v"].astype(F32).T.astype(BF16)
        ch["keep"] = jnp.exp(b_end)
    for ch in chunks:
        ch["o"] = jnp.dot(ch["attn"], ch["v"], preferred_element_type=F32)
        ch["kv"] = jnp.dot(ch["v_t"], ch["k_end"], preferred_element_type=F32)

    st = st_ref[...]
    for ch in chunks:
        ch["o"] = ch["o"] + lax.dot_general(ch["q_dec"], st.astype(BF16), NT_DIMS,
                                            preferred_element_type=F32)
        st = st * ch["keep"] + ch["kv"]
    st_ref[...] = st

    for ch in chunks:
        gate = _silu(g_ref[0, ch["sl"], :].astype(F32))
        o_ref[0, ch["sl"], :] = (_rms(ch["o"], gain) * gate).astype(o_ref.dtype)


def _hgrn_mixer(q, f, i, g, lower_bound, out_gain, ts=1024):
    B, S, W = q.shape
    dh = HGRN_HEAD_DIM
    ts = min(ts, S)
    seq = pl.BlockSpec((1, ts, dh), lambda b, h, s: (b, s, h))
    m = jnp.asarray(_hgrn_level_matrix(), BF16)
    return pl.pallas_call(
        functools.partial(_hgrn_kernel, n_chunks=ts // HGRN_C),
        grid=(B, W // dh, S // ts),
        in_specs=[seq, seq, seq, seq,
                  pl.BlockSpec((1, dh), lambda b, h, s: (0, h)),
                  pl.BlockSpec((1, dh), lambda b, h, s: (0, 0)),
                  pl.BlockSpec(m.shape, lambda b, h, s: (0, 0))],
        out_specs=seq,
        out_shape=jax.ShapeDtypeStruct((B, S, W), BF16),
        scratch_shapes=[pltpu.VMEM((dh, dh), F32)],
        compiler_params=_params("parallel", "parallel", "arbitrary"),
        name="hgrn_mixer",
    )(q, f, i, g, lower_bound.reshape(1, W), out_gain.reshape(1, dh), m)


SB_BLOCK = 256
SB_HALF = LANES


def _sb_kernel(q_ref, k_ref, v_ref, o_ref, *, n_blocks):
    BQ, HB = SB_BLOCK, SB_HALF
    heads = [slice(h * HEAD_DIM, (h + 1) * HEAD_DIM) for h in range(LANES // HEAD_DIM)]
    row = lax.broadcasted_iota(jnp.int32, (BQ, BQ), 0)
    col = lax.broadcasted_iota(jnp.int32, (BQ, BQ), 1)
    tr = lax.broadcasted_iota(jnp.int32, (HB, 2 * HB), 0)
    tc = lax.broadcasted_iota(jnp.int32, (HB, 2 * HB), 1)
    suffix = jnp.where((tc >= HB) | (tr > tc), 1.0, 0.0).astype(BF16)

    def step(qs, k0, accs, cs, diagonal):
        earlier = col < row
        zs = [lax.dot_general(q, k_ref[0, pl.ds(k0, BQ), hs], NT_DIMS, preferred_element_type=F32)
              for q, hs in zip(qs, heads)]
        lss, lhss = [], []
        for z in zs:
            ls = _log_sigmoid(z)
            lk = ls - z
            if diagonal:
                lk = jnp.where(earlier, lk, 0.0)
            lk = lk.astype(BF16)
            lss.append(ls)
            lhss.append(jnp.concatenate([lk[:, HB:], lk[:, :HB]], axis=0))
        ress = [jnp.dot(lhs, suffix, preferred_element_type=F32) for lhs in lhss]
        weights, new_cs = [], []
        for ls, res, c in zip(lss, ress, cs):
            res_r = res[:BQ]
            res_l = res[BQ:]
            stick_r = res_r[:, :HB] + c
            c = c + res_r[:, HB:]
            stick_l = res_l[:, :HB] + c
            new_cs.append(c + res_l[:, HB:])
            a = jnp.exp(ls + jnp.concatenate([stick_l, stick_r], axis=1))
            if diagonal:
                a = jnp.where(earlier, a, 0.0)
            weights.append(a.astype(BF16))
        accs = [acc + jnp.dot(a, v_ref[0, pl.ds(k0, BQ), hs], preferred_element_type=F32)
                for acc, a, hs in zip(accs, weights, heads)]
        return accs, new_cs

    def keep_going(c0, c1):
        return (jnp.max(jnp.maximum(c0, c1)) > F32_EXP_UNDERFLOW).astype(jnp.int32)

    def q_block(qb, _):
        q0 = pl.multiple_of(qb * BQ, BQ)
        qs = [q_ref[0, pl.ds(q0, BQ), hs] * (HEAD_DIM ** -0.5) for hs in heads]
        (acc0, acc1), (c0, c1) = step(qs, q0, [jnp.zeros((BQ, HEAD_DIM), F32)] * 2,
                                      [jnp.zeros((BQ, HB), F32)] * 2, True)

        def cond(state):
            return jnp.logical_and(state[0] <= qb, state[-1] > 0)

        def body(state):
            j, acc0, acc1, c0, c1, _ = state
            k0 = pl.multiple_of((qb - j) * BQ, BQ)
            (acc0, acc1), (c0, c1) = step(qs, k0, [acc0, acc1], [c0, c1], False)
            return j + 1, acc0, acc1, c0, c1, keep_going(c0, c1)

        state = lax.while_loop(cond, body, (jnp.int32(1), acc0, acc1, c0, c1, keep_going(c0, c1)))
        o_ref[0, pl.ds(q0, BQ), :] = jnp.concatenate(state[1:3], axis=1).astype(o_ref.dtype)
        return 0

    lax.fori_loop(0, n_blocks, q_block, 0)


def _sb_mixer(q, k, v):
    B, S, W = q.shape
    spec = pl.BlockSpec((1, S, LANES), lambda b, h: (b, 0, h))
    return pl.pallas_call(
        functools.partial(_sb_kernel, n_blocks=S // SB_BLOCK),
        grid=(B, W // LANES),
        in_specs=[spec, spec, spec],
        out_specs=spec,
        out_shape=jax.ShapeDtypeStruct((B, S, W), BF16),
        compiler_params=_params("parallel", "parallel"),
        name="sb_mixer",
    )(q, k, v)


def _alibi_slope(head):
    return float(np.exp2(-8.0 * (head + 1) / DIL_HEADS))


DIL_STEP_BLOCKS = 4


def _dil_kernel(q_ref, kp_ref, kc_ref, vp_ref, vc_ref, o_ref, l_ref, *, group, dil, n_back):
    BQ = DIL_BLOCK
    n = pl.program_id(2)
    row = lax.broadcasted_iota(jnp.int32, (BQ, BQ), 0)
    col = lax.broadcasted_iota(jnp.int32, (BQ, BQ), 1)
    off_c = row - col
    off_p = off_c + BQ
    valid_c = (off_c >= 0) & (off_c <= n_back)
    W = q_ref.shape[-1]
    head_of = lax.broadcasted_iota(jnp.int32, (BQ, W), 1) // HEAD_DIM
    ones = jnp.ones((BQ, LANES), BF16)
    items = []
    for sb in range(DIL_STEP_BLOCKS):
        rows = slice(sb * BQ, (sb + 1) * BQ)
        if sb == 0:
            valid_p = (off_p <= n_back) & (n > 0)
            k_prev, v_prev = kp_ref, vp_ref
            prev_rows = slice(0, BQ)
        else:
            valid_p = off_p <= n_back
            k_prev, v_prev = kc_ref, vc_ref
            prev_rows = slice((sb - 1) * BQ, sb * BQ)
        q_all = q_ref[0, rows, :] * (HEAD_DIM ** -0.5)
        blk = dict(rows=rows, valid_p=valid_p, q=q_all, k_c=kc_ref[0, rows, :], k_p=k_prev[0, prev_rows, :],
                   v_c=vc_ref[0, rows, :], v_p=v_prev[0, prev_rows, :])
        items += [(blk, hh) for hh in range(DIL_HEADS_PER_GROUP)]

    scores = []
    for blk, hh in items:
        q = jnp.where(head_of == hh, blk["q"], jnp.zeros_like(blk["q"]))
        scores.append((lax.dot_general(q, blk["k_c"], NT_DIMS, preferred_element_type=F32),
                       lax.dot_general(q, blk["k_p"], NT_DIMS, preferred_element_type=F32)))
    masked = []
    for (blk, hh), (s_c, s_p) in zip(items, scores):
        slope = _alibi_slope(group * DIL_HEADS_PER_GROUP + hh) * dil
        s_c = jnp.where(valid_c, s_c - slope * off_c.astype(F32), MASK_VALUE)
        s_p = jnp.where(blk["valid_p"], s_p - slope * off_p.astype(F32), MASK_VALUE)
        masked.append((s_c, s_p, jnp.max(jnp.maximum(s_c, s_p), axis=1, keepdims=True)))
    weights = []
    for (blk, hh), (s_c, s_p, m) in zip(items, masked):
        weights.append((jnp.where(valid_c, jnp.exp(s_c - m), 0.0).astype(BF16),
                        jnp.where(blk["valid_p"], jnp.exp(s_p - m), 0.0).astype(BF16)))
    sums = []
    for (blk, hh), (e_c, e_p) in zip(items, weights):
        o = (jnp.dot(e_c, blk["v_c"], preferred_element_type=F32)
             + jnp.dot(e_p, blk["v_p"], preferred_element_type=F32))
        den = (jnp.dot(e_c, ones, preferred_element_type=F32)
               + jnp.dot(e_p, ones, preferred_element_type=F32))
        sums.append((o, jnp.concatenate([den] * (W // LANES), axis=1)))
    for sb in range(DIL_STEP_BLOCKS):
        out = jnp.zeros((BQ, W), F32)
        lse = jnp.zeros((BQ, W), F32)
        for idx in range(sb * DIL_HEADS_PER_GROUP, (sb + 1) * DIL_HEADS_PER_GROUP):
            (blk, hh), (o, den), m = items[idx], sums[idx], masked[idx][2]
            out = jnp.where(head_of == hh, o / den, out)
            lse = jnp.where(head_of == hh, m + jnp.log(den), lse)
        o_ref[0, blk["rows"], :] = out.astype(o_ref.dtype)
        l_ref[0, blk["rows"], :] = lse


def _dil_group(q, k, v, group):
    window, dil = DIL_GROUPS[group]
    B, S, W = q.shape
    L = S // dil
    nb = L // DIL_BLOCK
    view = lambda a: a.reshape(B, L, dil * W)
    sbk = DIL_STEP_BLOCKS
    cur = pl.BlockSpec((1, sbk * DIL_BLOCK, DIL_OUT_W), lambda b, r, n: (b, n, r))
    prev = pl.BlockSpec((1, DIL_BLOCK, DIL_OUT_W), lambda b, r, n: (b, jnp.maximum(sbk * n - 1, 0), r))
    out = cur
    o, lse = pl.pallas_call(
        functools.partial(_dil_kernel, group=group, dil=dil, n_back=window // dil),
        grid=(B, dil, nb // sbk),
        in_specs=[cur, prev, cur, prev, cur],
        out_specs=[out, out],
        out_shape=[jax.ShapeDtypeStruct((B, L, dil * DIL_OUT_W), BF16),
                   jax.ShapeDtypeStruct((B, L, dil * DIL_OUT_W), F32)],
        compiler_params=_params("parallel", "parallel", "parallel"),
        name=f"dil_group{group}",
    )(view(q), view(k), view(k), view(v), view(v))
    return o.reshape(B, S, DIL_OUT_W), lse.reshape(B, S, DIL_OUT_W)


def _mem_attn_kernel(q_ref, k_ref, v_ref, o_ref):
    heads = [slice(hh * HEAD_DIM, (hh + 1) * HEAD_DIM) for hh in range(MEM_HEADS)]
    ss = [lax.dot_general(q_ref[0, :, hs] * (HEAD_DIM ** -0.5), k_ref[0, :, hs], NT_DIMS,
                          preferred_element_type=F32) for hs in heads]
    es = [jnp.exp(s - jnp.max(s, axis=1, keepdims=True)) for s in ss]
    os_ = [jnp.dot(e.astype(BF16), v_ref[0, :, hs], preferred_element_type=F32) for e, hs in zip(es, heads)]
    outs = [o / jnp.sum(e, axis=1, keepdims=True) for o, e in zip(os_, es)]
    o_ref[0] = jnp.concatenate(outs, axis=1).astype(o_ref.dtype)


def _mem_attention(q, mem_k, mem_v, tm=1024):
    B, S, W = q.shape
    M = mem_k.shape[1]
    tm = min(tm, S)
    tok = pl.BlockSpec((1, tm, W), lambda b, s: (b, s, 0))
    kv = pl.BlockSpec((1, M, W), lambda b, s: (b, 0, 0))
    return pl.pallas_call(
        _mem_attn_kernel,
        grid=(B, S // tm),
        in_specs=[tok, kv, kv],
        out_specs=tok,
        out_shape=jax.ShapeDtypeStruct((B, S, W), BF16),
        compiler_params=_params("parallel", "parallel"),
        name="mem_attention",
    )(q, mem_k, mem_v)


MOE_TM = 512
MOE_ALIGN = 16
MOE_R = 2 * MOE_TM + MOE_EXPERTS * MOE_ALIGN
MOE_NB = MOE_R // MOE_ALIGN
MOE_XW = D_MODEL + LANES
MOE_TME = 512
EXPERT_LANE0 = MOE_GROUPS


def _route(h, wr_ref, b_ref, tri_ref):
    tm = h.shape[0]
    h_hi = h.astype(BF16)
    h_lo = (h - h_hi.astype(F32)).astype(BF16)
    lg = jnp.dot(h_hi, wr_ref[...], preferred_element_type=F32)
    logits = (lg[:, :LANES] + lg[:, LANES:]
              + jnp.dot(h_lo, wr_ref[:, :LANES], preferred_element_type=F32) + b_ref[...])
    lane = lax.broadcasted_iota(jnp.int32, logits.shape, 1)
    neg = jnp.float32(-1e30)

    def first_argmax(vals):
        mx = jnp.max(vals, axis=1, keepdims=True)
        return mx, jnp.min(jnp.where(vals == mx, lane, LANES), axis=1, keepdims=True)

    glog = jnp.where(lane < MOE_GROUPS, logits, neg)
    gmax, gidx = first_argmax(glog)
    group_gate = 1.0 / jnp.sum(jnp.where(lane < MOE_GROUPS, jnp.exp(glog - gmax), 0.0),
                               axis=1, keepdims=True)
    lo = MOE_GROUPS + gidx * MOE_EXPERTS_PER_GROUP
    elog = jnp.where((lane >= lo) & (lane < lo + MOE_EXPERTS_PER_GROUP), logits, neg)
    m1, i1 = first_argmax(elog)
    m2, i2 = first_argmax(jnp.where(lane == i1, neg, elog))
    g1 = group_gate / (1.0 + jnp.exp(m2 - m1))
    g2 = group_gate - g1

    chosen = jnp.where((lane == i1) | (lane == i2), 1.0, 0.0)
    earlier = jnp.dot(tri_ref[...], chosen.astype(BF16), preferred_element_type=F32)
    counts = jnp.sum(chosen, axis=0, keepdims=True)
    padded = jnp.floor((counts + (MOE_ALIGN - 1)) * (1.0 / MOE_ALIGN)) * MOE_ALIGN
    ur = lax.broadcasted_iota(jnp.int32, (LANES, LANES), 0)
    uc = lax.broadcasted_iota(jnp.int32, (LANES, LANES), 1)
    before = jnp.where(ur < uc, 1.0, 0.0).astype(BF16)
    sub8 = lax.broadcasted_iota(jnp.int32, (8, LANES), 0)
    padded8 = jnp.broadcast_to(padded, (8, LANES))
    start8 = jnp.dot(padded8.astype(BF16), before, preferred_element_type=F32)
    pos = earlier + start8[0:1]
    r1 = jnp.sum(jnp.where(lane == i1, pos, 0.0), axis=1, keepdims=True)
    r2 = jnp.sum(jnp.where(lane == i2, pos, 0.0), axis=1, keepdims=True)
    g1_hi = g1.astype(BF16).astype(F32)
    g2_hi = g2.astype(BF16).astype(F32)
    far = LANES // 2
    tok = jnp.where(lane == 0, r1, jnp.where(lane == 1, r2, jnp.where(
        lane == i1, g1_hi, jnp.where(lane == i2, g2_hi, jnp.where(
            lane == i1 + far, g1 - g1_hi, jnp.where(lane == i2 + far, g2 - g2_hi, 0.0))))))
    tile_info = jnp.where(sub8 == 0, padded8, jnp.where(sub8 == 1, start8, 0.0))
    return tok, tok.T[:8], tile_info


def _mix_from_refs(mix_refs):
    if len(mix_refs) == 1:
        return mix_refs[0][...]
    o0, o1, o2, l0, l1, l2 = [r[...] for r in mix_refs]
    m = jnp.maximum(jnp.maximum(l0, l1), l2)
    w0, w1, w2 = jnp.exp(l0 - m), jnp.exp(l1 - m), jnp.exp(l2 - m)
    mix = (w0 * o0.astype(F32) + w1 * o1.astype(F32) + w2 * o2.astype(F32)) / (w0 + w1 + w2)
    return mix.astype(BF16)


def _out_proj_router_kernel(*refs, n_mix):
    mix_refs = refs[:n_mix]
    mem_ref, x_ref, w_ref, g_ref, wr_ref, b_ref, tri_ref = refs[n_mix:n_mix + 7]
    xo_ref, h_ref, tok_ref, tokt_ref, info_ref = refs[n_mix + 7:]
    mix = _mix_from_refs(mix_refs)
    nm = mix.shape[-1]
    x = (x_ref[...] + jnp.dot(mix, w_ref[:nm, :], preferred_element_type=F32)
         + jnp.dot(mem_ref[...], w_ref[nm:, :], preferred_element_type=F32))
    xo_ref[...] = x
    h = _rms(x, g_ref[...])
    h_ref[...] = h.astype(h_ref.dtype)
    tok, tok_t, info = _route(h, wr_ref, b_ref, tri_ref)
    tok_ref[...] = tok
    tokt_ref[0] = tok_t
    info_ref[0] = info


def _out_proj_router(mix_parts, mem_out, x2, w_out, gain, w_router, b_router):
    T, D = x2.shape
    tm = min(MOE_TM, T)
    nt = T // tm
    tok = lambda wd: pl.BlockSpec((tm, wd), lambda i: (i, 0))
    const = lambda a: pl.BlockSpec(a.shape, lambda i: (0,) * a.ndim)
    w_hi = w_router.astype(BF16)
    w_lo = (w_router - w_hi.astype(F32)).astype(BF16)
    wr = jnp.concatenate([w_hi, w_lo], axis=1)
    tri = jnp.asarray(np.tril(np.ones((tm, tm), np.float32), -1), BF16)
    ins = list(mix_parts) + [mem_out, x2]
    consts = [w_out, gain.reshape(1, D), wr, b_router, tri]
    return pl.pallas_call(
        functools.partial(_out_proj_router_kernel, n_mix=len(mix_parts)),
        grid=(nt,),
        in_specs=[tok(a.shape[1]) for a in ins] + [const(a) for a in consts],
        out_specs=[tok(D), tok(D), tok(LANES),
                   pl.BlockSpec((1, 8, tm), lambda i: (i, 0, 0)),
                   pl.BlockSpec((1, 8, LANES), lambda i: (i, 0, 0))],
        out_shape=[jax.ShapeDtypeStruct((T, D), F32), jax.ShapeDtypeStruct((T, D), BF16),
                   jax.ShapeDtypeStruct((T, LANES), F32), jax.ShapeDtypeStruct((nt, 8, tm), F32),
                   jax.ShapeDtypeStruct((nt, 8, LANES), F32)],
        compiler_params=_params("parallel"),
        name="out_proj_router",
    )(*ins, *consts)


def _moe_tables(info, n_ffn_tiles):
    E = MOE_EXPERTS
    lanes = slice(EXPERT_LANE0, EXPERT_LANE0 + E)
    rows_te = info[:, 0, lanes].astype(jnp.int32)
    start_te = info[:, 1, lanes].astype(jnp.int32)
    end_te = start_te + rows_te
    n_blk = end_te[:, -1] // MOE_ALIGN
    total = rows_te.sum(axis=0)
    cap = -(-total // MOE_TME) * MOE_TME
    cap_end = jnp.cumsum(cap)
    base = cap_end - cap
    run = jnp.cumsum(rows_te, axis=0) - rows_te
    rows = jnp.arange(MOE_NB, dtype=jnp.int32) * MOE_ALIGN
    e_of = jnp.sum(rows[None, :, None] >= end_te[:, None, :], axis=-1)
    in_e = e_of[:, :, None] == jnp.arange(E, dtype=jnp.int32)
    shift = base[None, :] + run - start_te
    dst_row = rows[None, :] + jnp.sum(jnp.where(in_e, shift[:, None, :], 0), axis=-1)
    blk_dst = dst_row // MOE_ALIGN
    n_used = (cap_end[-1] // MOE_TME).reshape(1)
    tile_row = jnp.arange(n_ffn_tiles, dtype=jnp.int32) * MOE_TME
    tile_expert = jnp.minimum(jnp.sum(tile_row[:, None] >= cap_end[None, :], axis=-1), E - 1)
    spare = jnp.full((1,), n_ffn_tiles * MOE_TME, jnp.int32)
    zero_start = jnp.concatenate([base + total, cap_end[-1:], spare]) // MOE_ALIGN
    zero_count = jnp.concatenate([cap - total, n_ffn_tiles * MOE_TME - cap_end[-1:],
                                  jnp.full((1,), 2 * MOE_R, jnp.int32)]) // MOE_ALIGN
    i32 = lambda a: a.astype(jnp.int32)
    return (i32(blk_dst.reshape(-1)), i32(n_blk), i32(tile_expert), i32(n_used),
            i32(zero_start), i32(zero_count))


def _tile_copies(copy, whole, t, slot):
    def start_all():
        def body(k, _):
            copy(t, slot, k).start()
            return 0
        lax.fori_loop(0, MOE_NB, body, 0)
    return start_all, (lambda: whole(slot).wait())


def _dispatch_kernel(dst_ref, nblk_ref, zs_ref, zc_ref, h_ref, tok_ref, tokt_ref, xs_ref,
                     sorted_ref, zero_ref, sems):
    i = pl.program_id(0)
    last = pl.num_programs(0) - 1
    slot = i % 2
    spare_blk = (xs_ref.shape[0] - 2 * MOE_R) // MOE_ALIGN

    def copy(t, s, k):
        blk = jnp.where(k < nblk_ref[t], dst_ref[t * MOE_NB + k], spare_blk + s * MOE_NB + k)
        return pltpu.make_async_copy(
            sorted_ref.at[s, pl.ds(pl.multiple_of(k * MOE_ALIGN, MOE_ALIGN), MOE_ALIGN), :],
            xs_ref.at[pl.ds(pl.multiple_of(blk * MOE_ALIGN, MOE_ALIGN), MOE_ALIGN), :], sems.at[s])

    def whole(s):
        return pltpu.make_async_copy(sorted_ref.at[s], xs_ref.at[pl.ds(0, MOE_R), :], sems.at[s])

    @pl.when(i >= 2)
    def _():
        _tile_copies(copy, whole, i - 2, slot)[1]()

    tm, D = h_ref.shape
    R = sorted_ref.shape[1]
    sub = lax.broadcasted_iota(jnp.int32, (R, tm), 0)
    p1 = sub == tokt_ref[0, 0:1, :].astype(jnp.int32)
    p2 = sub == tokt_ref[0, 1:2, :].astype(jnp.int32)
    both = jnp.where(p1 | p2, 1.0, 0.0).astype(BF16)
    rows = jnp.concatenate([h_ref[...], tok_ref[...].astype(BF16)], axis=1)
    sorted_ref[slot] = jnp.dot(both, rows, preferred_element_type=F32).astype(BF16)
    _tile_copies(copy, whole, i, slot)[0]()

    @pl.when(i == last)
    def _():
        @pl.when(i >= 1)
        def _():
            _tile_copies(copy, whole, i - 1, 1 - slot)[1]()

        _tile_copies(copy, whole, i, slot)[1]()
        zero_ref[...] = jnp.zeros_like(zero_ref)

        def zero_range(r, _):
            def zero_copy(k):
                dst = pl.multiple_of((zs_ref[r] + k) * MOE_ALIGN, MOE_ALIGN)
                return pltpu.make_async_copy(zero_ref, xs_ref.at[pl.ds(dst, MOE_ALIGN), :], sems.at[0])

            def start(k, _):
                zero_copy(k).start()
                return 0

            def wait(k, _):
                zero_copy(k).wait()
                return 0

            def batch(c, _):
                lo = c * MOE_NB
                hi = jnp.minimum(lo + MOE_NB, zc_ref[r])
                lax.fori_loop(lo, hi, start, 0)
                lax.fori_loop(lo, hi, wait, 0)
                return 0

            lax.fori_loop(0, (zc_ref[r] + MOE_NB - 1) // MOE_NB, batch, 0)
            return 0

        lax.fori_loop(0, zs_ref.shape[0], zero_range, 0)


def _ffn_kernel(te_ref, nu_ref, x_ref, w1_ref, w3_ref, w2_ref, o_ref):
    D = w1_ref.shape[1]

    @pl.when(pl.program_id(0) < nu_ref[0])
    def _():
        x = x_ref[:, :D]
        lanes = x_ref[:, D:].astype(F32)
        lane = lax.broadcasted_iota(jnp.int32, lanes.shape, 1)
        mine = EXPERT_LANE0 + te_ref[pl.program_id(0)]
        gate = jnp.sum(jnp.where((lane == mine) | (lane == mine + LANES // 2), lanes, 0.0),
                       axis=1, keepdims=True)
        a = jnp.dot(x, w1_ref[0], preferred_element_type=F32)
        b = jnp.dot(x, w3_ref[0], preferred_element_type=F32)
        hidden = (_silu(a) * b).astype(BF16)
        o_ref[...] = (jnp.dot(hidden, w2_ref[0], preferred_element_type=F32) * gate).astype(o_ref.dtype)

    @pl.when(pl.program_id(0) >= nu_ref[0])
    def _():
        o_ref[...] = jnp.zeros_like(o_ref)


def _combine_kernel(dst_ref, nblk_ref, x_ref, tok_ref, fg_ref, ys_ref, o_ref, sorted_ref, sems, *, final):
    i = pl.program_id(0)
    slot = i % 2
    tm = x_ref.shape[0]
    R = sorted_ref.shape[1]

    def copy(t, s, k):
        blk = jnp.where(k < nblk_ref[t], dst_ref[t * MOE_NB + k], 0)
        return pltpu.make_async_copy(
            ys_ref.at[pl.ds(pl.multiple_of(blk * MOE_ALIGN, MOE_ALIGN), MOE_ALIGN), :],
            sorted_ref.at[s, pl.ds(pl.multiple_of(k * MOE_ALIGN, MOE_ALIGN), MOE_ALIGN), :], sems.at[s])

    def whole(s):
        return pltpu.make_async_copy(ys_ref.at[pl.ds(0, MOE_R), :], sorted_ref.at[s], sems.at[s])

    @pl.when(i == 0)
    def _():
        _tile_copies(copy, whole, i, slot)[0]()

    @pl.when(i + 1 < pl.num_programs(0))
    def _():
        _tile_copies(copy, whole, i + 1, 1 - slot)[0]()

    _tile_copies(copy, whole, i, slot)[1]()
    lane = lax.broadcasted_iota(jnp.int32, (tm, R), 1)
    tok = tok_ref[...]
    pick = jnp.where((lane == tok[:, 0:1].astype(jnp.int32)) | (lane == tok[:, 1:2].astype(jnp.int32)),
                     1.0, 0.0).astype(BF16)
    y = x_ref[...] + jnp.dot(pick, sorted_ref[slot], preferred_element_type=F32)
    o_ref[...] = _rms(y, fg_ref[...]) if final else y


def _moe_experts(x2, h, tok, tok_t, info, w1, w3, w2, final_gain=None):
    T, D = x2.shape
    E, _, H = w1.shape
    tm = min(MOE_TM, T)
    nt = T // tm
    n_ffn = -(-(nt * MOE_R + E * (MOE_TME - 1)) // MOE_TME)
    blk_dst, n_blk, tile_expert, n_used, zero_start, zero_count = _moe_tables(info, n_ffn)
    rows = n_ffn * MOE_TME

    xs = pl.pallas_call(
        _dispatch_kernel,
        grid_spec=pltpu.PrefetchScalarGridSpec(
            num_scalar_prefetch=4, grid=(nt,),
            in_specs=[pl.BlockSpec((tm, D), lambda i, *_: (i, 0)),
                      pl.BlockSpec((tm, LANES), lambda i, *_: (i, 0)),
                      pl.BlockSpec((1, 8, tm), lambda i, *_: (i, 0, 0))],
            out_specs=pl.BlockSpec(memory_space=pl.ANY),
            scratch_shapes=[pltpu.VMEM((2, MOE_R, MOE_XW), BF16), pltpu.VMEM((MOE_ALIGN, MOE_XW), BF16),
                            pltpu.SemaphoreType.DMA((2,))]),
        out_shape=jax.ShapeDtypeStruct((rows + 2 * MOE_R, MOE_XW), BF16),
        compiler_params=_params("arbitrary"),
        name="moe_dispatch",
    )(blk_dst, n_blk, zero_start, zero_count, h, tok, tok_t)

    last = lambda n, nu: jnp.minimum(n, nu[0] - 1)
    ys = pl.pallas_call(
        _ffn_kernel,
        grid_spec=pltpu.PrefetchScalarGridSpec(
            num_scalar_prefetch=2, grid=(n_ffn,),
            in_specs=[pl.BlockSpec((MOE_TME, MOE_XW), lambda n, te, nu: (last(n, nu), 0)),
                      pl.BlockSpec((1, D, H), lambda n, te, nu: (te[n], 0, 0)),
                      pl.BlockSpec((1, D, H), lambda n, te, nu: (te[n], 0, 0)),
                      pl.BlockSpec((1, H, D), lambda n, te, nu: (te[n], 0, 0))],
            out_specs=pl.BlockSpec((MOE_TME, D), lambda n, te, nu: (n, 0))),
        out_shape=jax.ShapeDtypeStruct((rows, D), BF16),
        compiler_params=_params("arbitrary"),
        name="moe_ffn",
    )(tile_expert, n_used, xs, w1, w3, w2)

    fg = jnp.ones((1, D), F32) if final_gain is None else final_gain.reshape(1, D).astype(F32)
    return pl.pallas_call(
        functools.partial(_combine_kernel, final=final_gain is not None),
        grid_spec=pltpu.PrefetchScalarGridSpec(
            num_scalar_prefetch=2, grid=(nt,),
            in_specs=[pl.BlockSpec((tm, D), lambda i, *_: (i, 0)),
                      pl.BlockSpec((tm, LANES), lambda i, *_: (i, 0)),
                      pl.BlockSpec((1, D), lambda i, *_: (0, 0)),
                      pl.BlockSpec(memory_space=pl.ANY)],
            out_specs=pl.BlockSpec((tm, D), lambda i, *_: (i, 0)),
            scratch_shapes=[pltpu.VMEM((2, MOE_R, D), BF16), pltpu.SemaphoreType.DMA((2,))]),
        out_shape=jax.ShapeDtypeStruct((T, D), F32),
        compiler_params=_params("arbitrary"),
        name="moe_combine",
    )(blk_dst, n_blk, x2, tok, fg, ys)


def _segments(widths):
    starts = np.cumsum([0] + list(widths[:-1]))
    return [(int(s), int(w)) for s, w in zip(starts, widths)]


def kernel(x, mem, norm1, norm2, mem_norm, final_norm, mem_w_kv, hgrn_w_in, hgrn_onorm, hgrn_lb_logits, hgrn_w_out, sb_w_in, sb_w_out, dil_w_in, dil_w_out, moe_w_group, moe_b_group, moe_w_expert, moe_b_expert, moe_w1, moe_w3, moe_w2):
    B, S, D = x.shape
    depth = norm1.shape[0]
    T = B * S
    x2 = x.reshape(T, D)

    lb_w = jax.nn.softmax(hgrn_lb_logits.astype(F32), axis=0)
    lower_bounds = jnp.maximum(jnp.cumsum(lb_w, axis=0) - lb_w[:1], 0.0)

    M = mem.shape[1]
    w_kv = mem_w_kv.transpose(1, 0, 2).reshape(D, depth * 2 * MEM_W).astype(BF16)
    mem_kv = _norm_proj(mem.reshape(B * M, D), mem_norm, w_kv, _segments([MEM_W] * (2 * depth)),
                        [BF16] * (2 * depth))

    for layer in range(depth):
        kind, j = layer % N_MIXERS, layer // N_MIXERS
        mem_k = mem_kv[2 * layer].reshape(B, M, MEM_W)
        mem_v = mem_kv[2 * layer + 1].reshape(B, M, MEM_W)
        seq = lambda a: a.reshape(B, S, a.shape[-1])
        if kind == 0:
            q, f, i, g, mq = _norm_proj(x2, norm1[layer], hgrn_w_in[j].astype(BF16),
                                        _segments([MIX_W] * 4 + [MEM_W]), [BF16, F32, BF16, BF16, BF16])
            mix = _hgrn_mixer(seq(q), seq(f), seq(i), seq(g), lower_bounds[layer], hgrn_onorm[j])
            mix_parts, w_out = [mix.reshape(T, MIX_W)], hgrn_w_out[j]
        elif kind == 1:
            q, k, v, mq = _norm_proj(x2, norm1[layer], sb_w_in[j].astype(BF16),
                                     _segments([MIX_W] * 3 + [MEM_W]), [BF16] * 4)
            mix = _sb_mixer(seq(q), seq(k), seq(v))
            mix_parts, w_out = [mix.reshape(T, MIX_W)], sb_w_out[j]
        else:
            n_grp = len(DIL_GROUPS)
            *qkv, mq = _norm_proj(x2, norm1[layer], dil_w_in[j].astype(BF16),
                                  _segments([DIL_OUT_W] * (3 * n_grp) + [MEM_W]), [BF16] * (3 * n_grp + 1))
            parts = [_dil_group(seq(qkv[grp]), seq(qkv[n_grp + grp]), seq(qkv[2 * n_grp + grp]), grp)
                     for grp in range(n_grp)]
            mix_parts = ([o.reshape(T, DIL_OUT_W) for o, _ in parts]
                         + [l.reshape(T, DIL_OUT_W) for _, l in parts])
            w_out = dil_w_out[j]
        mem_out = _mem_attention(seq(mq), mem_k, mem_v).reshape(T, MEM_W)

        pad = LANES - MOE_GROUPS - MOE_EXPERTS
        w_router = jnp.pad(jnp.concatenate([moe_w_group[layer], moe_w_expert[layer]], axis=1),
                           ((0, 0), (0, pad))).astype(F32)
        b_router = jnp.pad(jnp.concatenate([moe_b_group[layer], moe_b_expert[layer]]),
                           (0, pad)).astype(F32).reshape(1, LANES)
        x2, h, tok, tok_t, info = _out_proj_router(mix_parts, mem_out, x2, w_out.astype(BF16),
                                                   norm2[layer], w_router, b_router)
        x2 = _moe_experts(x2, h, tok, tok_t, info, moe_w1[layer].astype(BF16),
                          moe_w3[layer].astype(BF16), moe_w2[layer].astype(BF16),
                          final_gain=final_norm if layer == depth - 1 else None)

    return x2.reshape(B, S, D)
```
